```python
import jax, jax.numpy as jnp
from jax import lax
import numpy as np


D_MODEL = 1024
BATCH = 16
SEQ = 4096
DEPTH = 1

HEAD_DIM = 64
NSA_HEADS = 8
NSA_KV_HEADS = 2
SWA_HEADS = 8
SWA_KV_HEADS = 2
N_HEADS = NSA_HEADS + SWA_HEADS
MIX_WIDTH = N_HEADS * HEAD_DIM
CMP_LEN = 32
CMP_STRIDE = 16
CMP_HIDDEN = 128
SEL_LEN = 64
N_SEL = 16
NSA_WINDOW = 512
NSA_Q_BLOCK = 64
SWA_WINDOW = 128
SWA_Q_BLOCK = 128
D_FF = 2816
CONV_W = 3
PLE_DIM = 256
RMS_EPS = 1e-6
NEG_INF = -1e30
FORCE_BONUS = 1e6

NSA_Q_W = NSA_HEADS * HEAD_DIM
NSA_KV_W = NSA_KV_HEADS * HEAD_DIM
NSA_GATE_W = 3 * NSA_HEADS
SWA_Q_W = SWA_HEADS * HEAD_DIM
SWA_KV_W = SWA_KV_HEADS * HEAD_DIM
IN_SPLITS = [NSA_Q_W] + [NSA_KV_W] * 6 + [NSA_GATE_W, SWA_Q_W, SWA_KV_W, SWA_KV_W]
IN_WIDTH = sum(IN_SPLITS)
IN_SPLIT_POINTS = [int(v) for v in np.cumsum(IN_SPLITS)[:-1]]

kernel_name = 'hybrid_nsa_swa_sink_convffn_block'


def rmsnorm(x, g):
    xf = x.astype(jnp.float32)
    y = xf * lax.rsqrt(jnp.mean(xf * xf, axis=-1, keepdims=True) + RMS_EPS)
    return (y * g.astype(jnp.float32)).astype(x.dtype)


def alibi_slopes():
    h = jnp.arange(N_HEADS, dtype=jnp.float32)
    return jnp.exp2(-8.0 * (h + 1.0) / N_HEADS)


def compress_kv(kv, pe, w1, w2):
    B, S, G, hd = kv.shape
    r = CMP_LEN // CMP_STRIDE
    seg = kv.reshape(B, S // CMP_STRIDE, CMP_STRIDE, G, hd)
    n_cmp = S // CMP_STRIDE - r + 1
    blocks = jnp.concatenate([seg[:, i:i + n_cmp] for i in range(r)], axis=2)
    blocks = blocks + pe.astype(kv.dtype)[:, None, :]
    flat = jnp.transpose(blocks, (0, 1, 3, 2, 4)).reshape(B, n_cmp, G, CMP_LEN * hd)
    hid = jax.nn.gelu(flat @ w1.astype(kv.dtype))
    return hid @ w2.astype(kv.dtype)


def nsa_attention(q, k_cmp, v_cmp, k_slc, v_slc, k_win, v_win, gates, slopes):
    B, S, H, hd = q.shape
    G = k_slc.shape[2]
    R = H // G
    C = NSA_Q_BLOCK
    n_cmp = k_cmp.shape[1]
    n_sb = S // SEL_LEN
    n_top = min(N_SEL, n_sb)
    scale = hd ** -0.5
    slopes_g = slopes.reshape(G, R)
    cmp_start = jnp.arange(n_cmp) * CMP_STRIDE
    cmp_end = cmp_start + CMP_LEN - 1
    sel_start = jnp.arange(n_sb) * SEL_LEN
    overlap = ((cmp_start[:, None] < sel_start[None, :] + SEL_LEN)
               & (cmp_end[:, None] >= sel_start[None, :])).astype(jnp.float32)
    kb = jnp.transpose(k_slc.reshape(B, n_sb, SEL_LEN, G, hd), (0, 3, 1, 2, 4))
    vb = jnp.transpose(v_slc.reshape(B, n_sb, SEL_LEN, G, hd), (0, 3, 1, 2, 4))
    pad = ((0, 0), (NSA_WINDOW, 0), (0, 0), (0, 0))
    k_wp = jnp.pad(k_win, pad)
    v_wp = jnp.pad(v_win, pad)
    bi = jnp.arange(B)[:, None, None, None]
    gi = jnp.arange(G)[None, :, None, None]
    blk = jnp.arange(n_sb)

    def one_block(c):
        t0 = c * C
        t = t0 + jnp.arange(C)
        qc = lax.dynamic_slice_in_dim(q, t0, C, axis=1).reshape(B, C, G, R, hd)
        gc = lax.dynamic_slice_in_dim(gates, t0, C, axis=1).reshape(B, C, G, R, 3)
        dist_c = (t[:, None] - cmp_end[None, :]).astype(jnp.float32)
        valid_c = dist_c >= 0
        s_c = jnp.einsum('bcgrd,bngd->bgrcn', qc, k_cmp).astype(jnp.float32) * scale
        s_c = jnp.where(valid_c, s_c - slopes_g[:, :, None, None] * dist_c, NEG_INF)
        p_c = jax.nn.softmax(s_c, axis=-1) * valid_c
        o_c = jnp.einsum('bgrcn,bngd->bcgrd', p_c.astype(v_cmp.dtype), v_cmp)
        imp = jnp.einsum('bgrcn,nj->bgcj', p_c, overlap)
        jt = t // SEL_LEN
        forced = (blk[None, :] == 0) | (blk[None, :] == jt[:, None]) | (blk[None, :] == jt[:, None] - 1)
        imp = jnp.where(forced, FORCE_BONUS, imp)
        imp = jnp.where(blk[None, :] <= jt[:, None], imp, NEG_INF)
        _, idx = lax.top_k(imp, n_top)
        ks = kb[bi, gi, idx]
        vs = vb[bi, gi, idx]
        pos = idx[..., None] * SEL_LEN + jnp.arange(SEL_LEN)
        dist_s = (t[:, None, None] - pos).astype(jnp.float32)[:, :, None]
        s_s = jnp.einsum('bcgrd,bgcnld->bgrcnl', qc, ks).astype(jnp.float32) * scale
        s_s = jnp.where(dist_s >= 0, s_s - slopes_g[None, :, :, None, None, None] * dist_s, NEG_INF)
        p_s = jax.nn.softmax(s_s, axis=(-2, -1))
        o_s = jnp.einsum('bgrcnl,bgcnld->bcgrd', p_s.astype(vs.dtype), vs)
        kw = lax.dynamic_slice_in_dim(k_wp, t0, C + NSA_WINDOW, axis=1)
        vw = lax.dynamic_slice_in_dim(v_wp, t0, C + NSA_WINDOW, axis=1)
        s_pos = t0 - NSA_WINDOW + jnp.arange(C + NSA_WINDOW)
        dist_w = t[:, None] - s_pos[None, :]
        valid_w = (dist_w >= 0) & (dist_w < NSA_WINDOW) & (s_pos[None, :] >= 0)
        s_w = jnp.einsum('bcgrd,bkgd->bgrck', qc, kw).astype(jnp.float32) * scale
        s_w = jnp.where(valid_w, s_w - slopes_g[:, :, None, None] * dist_w.astype(jnp.float32), NEG_INF)
        p_w = jax.nn.softmax(s_w, axis=-1)
        o_w = jnp.einsum('bgrck,bkgd->bcgrd', p_w.astype(vw.dtype), vw)
        o = gc[..., 0:1] * o_c + gc[..., 1:2] * o_s + gc[..., 2:3] * o_w
        return o.reshape(B, C, H * hd)

    out = lax.map(one_block, jnp.arange(S // C))
    return jnp.transpose(out, (1, 0, 2, 3)).reshape(B, S, H * hd)


def swa_attention(q, k, v, sinks, slopes):
    B, S, H, hd = q.shape
    G = k.shape[2]
    R = H // G
    C = SWA_Q_BLOCK
    n_q = S // C
    scale = hd ** -0.5
    qb = q.reshape(B, n_q, C, G, R, hd)
    pad = ((0, 0), (C, 0), (0, 0), (0, 0))
    kp = jnp.pad(k, pad).reshape(B, n_q + 1, C, G, hd)
    vp = jnp.pad(v, pad).reshape(B, n_q + 1, C, G, hd)
    kb = jnp.concatenate([kp[:, :-1], kp[:, 1:]], axis=2)
    vb = jnp.concatenate([vp[:, :-1], vp[:, 1:]], axis=2)
    i = jnp.arange(C)[:, None]
    j = jnp.arange(2 * C)[None, :]
    dist = C + i - j
    kpos = jnp.arange(n_q)[:, None, None] * C - C + j[None]
    valid = (dist >= 0) & (dist < SWA_WINDOW) & (kpos >= 0)
    slopes_g = slopes.reshape(G, R)
    s = jnp.einsum('bnqgrd,bnkgd->bngrqk', qb, kb).astype(jnp.float32) * scale
    s = s - slopes_g[:, :, None, None] * dist.astype(jnp.float32)
    s = jnp.where(valid[:, None, None], s, NEG_INF)
    sg = sinks.astype(jnp.float32).reshape(G, R)[:, :, None, None]
    m = jnp.maximum(jnp.max(s, axis=-1, keepdims=True), sg)
    e = jnp.exp(s - m)
    prob = e / (jnp.sum(e, axis=-1, keepdims=True) + jnp.exp(sg - m))
    o = jnp.einsum('bngrqk,bnkgd->bnqgrd', prob.astype(vb.dtype), vb)
    return o.reshape(B, S, H * hd)


def causal_dwconv(a, w, b):
    y = lax.conv_general_dilated(a, w[:, None, :].astype(a.dtype), window_strides=(1,),
                                 padding=[(CONV_W - 1, 0)],
                                 dimension_numbers=('NWC', 'WIO', 'NWC'),
                                 feature_group_count=a.shape[-1])
    return y + b.astype(a.dtype)


def setup_inputs(seed: int = 0) -> dict:
    key = jax.random.key(seed)
    ks = jax.random.split(key, 24)
    f32 = jnp.float32
    L = DEPTH

    def nrm(k, shape, scale):
        return jax.random.normal(k, shape, f32) * scale

    def gain(k):
        return 1.0 + 0.02 * jax.random.normal(k, (L, D_MODEL), f32)

    return {
        'x': nrm(ks[0], (BATCH, SEQ, D_MODEL), 1.0),
        'p': nrm(ks[1], (DEPTH, BATCH, SEQ, PLE_DIM), 1.0),
        'attn_pre_g': gain(ks[2]),
        'w_in': nrm(ks[3], (L, D_MODEL, IN_WIDTH), D_MODEL ** -0.5),
        'cmp_pe_k': nrm(ks[4], (L, CMP_LEN, HEAD_DIM), 0.02),
        'cmp_w1_k': nrm(ks[5], (L, CMP_LEN * HEAD_DIM, CMP_HIDDEN), (CMP_LEN * HEAD_DIM) ** -0.5),
        'cmp_w2_k': nrm(ks[6], (L, CMP_HIDDEN, HEAD_DIM), CMP_HIDDEN ** -0.5),
        'cmp_pe_v': nrm(ks[7], (L, CMP_LEN, HEAD_DIM), 0.02),
        'cmp_w1_v': nrm(ks[8], (L, CMP_LEN * HEAD_DIM, CMP_HIDDEN), (CMP_LEN * HEAD_DIM) ** -0.5),
        'cmp_w2_v': nrm(ks[9], (L, CMP_HIDDEN, HEAD_DIM), CMP_HIDDEN ** -0.5),
        'sinks': nrm(ks[10], (L, SWA_HEADS), 1.0),
        'w_o': nrm(ks[11], (L, MIX_WIDTH, D_MODEL), MIX_WIDTH ** -0.5),
        'attn_post_g': gain(ks[12]),
        'mlp_pre_g': gain(ks[13]),
        'w_gate_up': nrm(ks[14], (L, D_MODEL, 2 * D_FF), D_MODEL ** -0.5),
        'conv_w': nrm(ks[15], (L, CONV_W, D_FF), CONV_W ** -0.5),
        'conv_b': nrm(ks[16], (L, D_FF), 0.01),
        'w_down': nrm(ks[17], (L, D_FF, D_MODEL), D_FF ** -0.5),
        'mlp_post_g': gain(ks[18]),
        'w_ple': nrm(ks[19], (L, PLE_DIM, D_MODEL), PLE_DIM ** -0.5),
        'w_ple_gate': nrm(ks[20], (L, D_MODEL, D_MODEL), D_MODEL ** -0.5),
    }


def reference(x, p, attn_pre_g, w_in, cmp_pe_k, cmp_w1_k, cmp_w2_k, cmp_pe_v, cmp_w1_v, cmp_w2_v,
              sinks, w_o, attn_post_g, mlp_pre_g, w_gate_up, conv_w, conv_b, w_down, mlp_post_g,
              w_ple, w_ple_gate):
    B, S, _ = x.shape
    slopes = alibi_slopes()
    swa_slopes = slopes[:SWA_HEADS]
    nsa_slopes = slopes[SWA_HEADS:]
    for i in range(DEPTH):
        h = rmsnorm(x, attn_pre_g[i])
        proj = h @ w_in[i].astype(h.dtype)
        (q_n, kc, vc, ksl, vsl, kwn, vwn, g_n, q_s, k_s, v_s) = jnp.split(proj, IN_SPLIT_POINTS, axis=-1)
        kv_shape = (B, S, NSA_KV_HEADS, HEAD_DIM)
        k_cmp = compress_kv(kc.reshape(kv_shape), cmp_pe_k[i], cmp_w1_k[i], cmp_w2_k[i])
        v_cmp = compress_kv(vc.reshape(kv_shape), cmp_pe_v[i], cmp_w1_v[i], cmp_w2_v[i])
        gates = jax.nn.sigmoid(g_n).reshape(B, S, NSA_HEADS, 3)
        o_nsa = nsa_attention(q_n.reshape(B, S, NSA_HEADS, HEAD_DIM), k_cmp, v_cmp,
                              ksl.reshape(kv_shape), vsl.reshape(kv_shape),
                              kwn.reshape(kv_shape), vwn.reshape(kv_shape), gates, nsa_slopes)
        o_swa = swa_attention(q_s.reshape(B, S, SWA_HEADS, HEAD_DIM),
                              k_s.reshape(B, S, SWA_KV_HEADS, HEAD_DIM),
                              v_s.reshape(B, S, SWA_KV_HEADS, HEAD_DIM), sinks[i], swa_slopes)
        mix = jnp.concatenate([o_nsa, o_swa], axis=-1) @ w_o[i].astype(x.dtype)
        x = x + rmsnorm(mix, attn_post_g[i])
        h = rmsnorm(x, mlp_pre_g[i])
        gu = h @ w_gate_up[i].astype(h.dtype)
        a, u = jnp.split(gu, [D_FF], axis=-1)
        a = causal_dwconv(a, conv_w[i], conv_b[i])
        y = (jax.nn.gelu(a, approximate=True) * u) @ w_down[i].astype(h.dtype)
        x = x + rmsnorm(y, mlp_post_g[i])
        e = p[i] @ w_ple[i].astype(x.dtype)
        x = x + e * jax.nn.sigmoid(x @ w_ple_gate[i].astype(x.dtype))
    return x
```

```python
import functools

import numpy as np
import jax
import jax.numpy as jnp
from jax import lax
from jax.experimental import pallas as pl
from jax.experimental.pallas import tpu as pltpu

F32 = jnp.float32
BF16 = jnp.bfloat16

D_MODEL = 1024
HEAD_DIM = 64
NSA_HEADS = 8
SWA_HEADS = 8
N_HEADS = NSA_HEADS + SWA_HEADS
CMP_LEN = 32
CMP_STRIDE = 16
CMP_HIDDEN = 128
SEL_LEN = 64
N_SEL = 16
NSA_WINDOW = 512
SWA_WINDOW = 128
D_FF = 2816
CONV_W = 3
PLE_DIM = 256
RMS_EPS = 1e-6
NEG_INF = -1e30
FORCE_BONUS = 1e6

LANES = 128
MASK_BIG = 2.0 ** 100
M_INIT = -3.0e38
GROUPS = 2
REP = 4
HEAD_PERM = (0, 4, 1, 5, 2, 6, 3, 7)
N_ALIBI_TERMS = 3

VMEM_LIMIT = 56 * 1024 * 1024


def _rms(x, g):
    return x * lax.rsqrt(jnp.mean(x * x, axis=-1, keepdims=True) + RMS_EPS) * g


def _gelu_tanh(x):
    c = np.sqrt(2.0 / np.pi).astype(np.float32)
    return x * (0.5 * (1.0 + jnp.tanh(c * (x + 0.044715 * (x * x * x)))))


def _dot(a, b):
    return jnp.dot(a, b, preferred_element_type=F32)


def _dot_nt(a, b):
    return lax.dot_general(a, b, (((1,), (1,)), ((), ())), preferred_element_type=F32)


def _rep_lanes(a, n):
    return a if n == 1 else jnp.concatenate([a] * n, axis=1)


def _params(sem):
    return pltpu.CompilerParams(dimension_semantics=sem, vmem_limit_bytes=VMEM_LIMIT)


def _inproj_body(x_ref, g_ref, w_ref, qn_ref, kc_ref, vc_ref, nkv_ref, swa_ref, gates_ref):
    h = _rms(x_ref[...], g_ref[...]).astype(BF16)
    proj = _dot(h, w_ref[...])
    qn_ref[...] = proj[:, 0:512].astype(BF16)
    kc_ref[...] = proj[:, 512:640].astype(BF16)
    vc_ref[...] = proj[:, 640:768].astype(BF16)
    nkv_ref[...] = proj[:, 768:1280].astype(BF16)
    swa_ref[...] = proj[:, 1280:2048].astype(BF16)
    gates_ref[...] = jax.nn.sigmoid(proj[:, 2048:2176])


def _inproj(x2d, g, w_cat, tm):
    T = x2d.shape[0]
    wn = w_cat.shape[1]
    row = lambda n: pl.BlockSpec((tm, n), lambda i: (i, 0))
    full = lambda a: pl.BlockSpec(a.shape, lambda i: (0,) * a.ndim)
    return pl.pallas_call(
        _inproj_body,
        grid=(T // tm,),
        in_specs=[row(D_MODEL), full(g), full(w_cat)],
        out_specs=[row(512), row(128), row(128), row(512), row(768), row(128)],
        out_shape=[
            jax.ShapeDtypeStruct((T, 512), BF16),
            jax.ShapeDtypeStruct((T, 128), BF16),
            jax.ShapeDtypeStruct((T, 128), BF16),
            jax.ShapeDtypeStruct((T, 512), BF16),
            jax.ShapeDtypeStruct((T, 768), BF16),
            jax.ShapeDtypeStruct((T, 128), F32),
        ],
        compiler_params=_params(("parallel",)),
        name="inproj",
    )(x2d, g, w_cat)


def _compress_body(kc_ref, vc_ref,
                   pea_k, peb_k, w1a_k, w1b_k, w2_k,
                   pea_v, peb_v, w1a_v, w1b_v, w2_v,
                   kcmp_ref, vcmp_ref):
    def one(seg_ref, pea, peb, w1a, w1b, w2, out_ref):
        seg = seg_ref[...].astype(F32)
        ns = seg.shape[0]
        ha = _dot((seg + pea[...]).astype(BF16), w1a[...])
        hb = _dot((seg + peb[...]).astype(BF16), w1b[...])
        hid = ha + pltpu.roll(hb, ns - 1, 0)
        act = _gelu_tanh(hid).astype(BF16)
        out_ref[...] = _dot(act, w2[...]).astype(BF16)

    one(kc_ref, pea_k, peb_k, w1a_k, w1b_k, w2_k, kcmp_ref)
    one(vc_ref, pea_v, peb_v, w1a_v, w1b_v, w2_v, vcmp_ref)


def _compress_weights(pe, w1, w2):
    half = CMP_LEN // 2
    eye = jnp.eye(GROUPS, dtype=F32)
    w1r = w1.reshape(CMP_LEN, HEAD_DIM, CMP_HIDDEN)

    def expand(w):
        return jnp.einsum("ldj,pg->lpdgj", w, eye).reshape(half * GROUPS * HEAD_DIM, GROUPS * CMP_HIDDEN)

    def pe_row(p):
        return jnp.broadcast_to(p[:, None, :], (half, GROUPS, HEAD_DIM)).reshape(1, half * GROUPS * HEAD_DIM)

    w2e = jnp.einsum("jd,pg->pjgd", w2, eye).reshape(GROUPS * CMP_HIDDEN, GROUPS * HEAD_DIM)
    return (pe_row(pe[:half]), pe_row(pe[half:]),
            expand(w1r[:half]).astype(BF16), expand(w1r[half:]).astype(BF16), w2e.astype(BF16))


def _compress(kc_seg, vc_seg, wk, wv):
    B, NS, WD = kc_seg.shape
    seg = pl.BlockSpec((None, NS, WD), lambda b: (b, 0, 0))
    full = lambda a: pl.BlockSpec(a.shape, lambda b: (0,) * a.ndim)
    out = pl.BlockSpec((None, NS, 128), lambda b: (b, 0, 0))
    return pl.pallas_call(
        _compress_body,
        grid=(B,),
        in_specs=[seg, seg] + [full(a) for a in wk] + [full(a) for a in wv],
        out_specs=[out, out],
        out_shape=[jax.ShapeDtypeStruct((B, NS, 128), BF16)] * 2,
        compiler_params=_params(("parallel",)),
        name="compress",
    )(kc_seg, vc_seg, *wk, *wv)


def _group_rows(q_all, g, C):
    lane = lax.broadcasted_iota(jnp.int32, (C, LANES), 1)
    keep = (lane < HEAD_DIM) if g == 0 else (lane >= HEAD_DIM)
    zero = jnp.zeros((C, LANES), q_all.dtype)
    return jnp.concatenate(
        [jnp.where(keep, q_all[:, r * LANES:(r + 1) * LANES], zero) for r in range(REP)], axis=0)


def _nsa_body(slopes_ref, qn_ref, gates_ref, kcmp_ref, vcmp_ref, ksl_ref, vsl_ref, kwn_ref, vwn_ref,
              kaug_ref, qfeat_ref, ovt_ref, out_ref,
              qa_ref, m_ref, l_ref, acc_ref, imp_ref, *, C, TK, W, NSB, NC, n_top):
    t0 = pl.program_id(1) * C
    R4 = REP * C
    q_all = qn_ref[...]
    qg = [_group_rows(q_all, g, C) for g in range(GROUPS)]

    tq_c = t0 + lax.broadcasted_iota(jnp.int32, (C, NC), 0)
    cend = lax.broadcasted_iota(jnp.int32, (C, NC), 1) * CMP_STRIDE + (CMP_LEN - 1)
    valid_c = cend <= tq_c
    cend_rel = (cend - t0).astype(F32)
    o_cmp = []
    for g in range(GROUPS):
        sc = _dot_nt(qg[g], kcmp_ref[...])
        psum = jnp.zeros((C, NC), F32)
        probs = []
        for r in range(REP):
            s = sc[r * C:(r + 1) * C] + slopes_ref[g * REP + r] * cend_rel
            s = jnp.where(valid_c, s, NEG_INF)
            e = jnp.exp(s - jnp.max(s, axis=-1, keepdims=True))
            p = e * (1.0 / jnp.sum(e, axis=-1, keepdims=True))
            p = jnp.where(valid_c, p, 0.0)
            psum = psum + p
            probs.append(p.astype(BF16))
        o_cmp.append(_dot(jnp.concatenate(probs, axis=0), vcmp_ref[...]))

        ovt = ovt_ref[...]
        p_hi = psum.astype(BF16)
        r1 = psum - p_hi.astype(F32)
        p_mid = r1.astype(BF16)
        p_lo = (r1 - p_mid.astype(F32)).astype(BF16)
        imp = _dot_nt(ovt, p_hi) + _dot_nt(ovt, p_mid) + _dot_nt(ovt, p_lo)

        jblk = lax.broadcasted_iota(jnp.int32, (NSB, C), 0)
        jt = lax.shift_right_logical(t0 + lax.broadcasted_iota(jnp.int32, (NSB, C), 1), 6)
        forced = (jblk == 0) | (jblk == jt) | (jblk == jt - 1)
        visible = jblk <= jt
        imp = jnp.where(visible, jnp.where(forced, FORCE_BONUS, imp), NEG_INF)
        imp_ref[...] = imp
        nv = NSB // 8
        cnt = [jnp.zeros((8, C), F32) for _ in range(nv)]
        for i in range(NSB):
            row = jnp.broadcast_to(imp_ref[i:i + 1, :], (8, C))
            for v in range(nv):
                blk = imp[8 * v:8 * v + 8]
                if i < 8 * v:
                    before = row >= blk
                elif i >= 8 * v + 8:
                    before = row > blk
                else:
                    before = (row > blk) | ((row == blk) & (jblk[8 * v:8 * v + 8] > i))
                cnt[v] = cnt[v] + jnp.where(before, 1.0, 0.0)
        rank = jnp.concatenate(cnt, axis=0)
        sel_t = jnp.where(visible & (rank < n_top), 0.0, -MASK_BIG)
        sel_t = jnp.concatenate([sel_t, jnp.zeros((LANES - NSB, C), F32)], axis=0)
        selb = sel_t.T
        for r in range(REP):
            feat = qfeat_ref[g * REP + r:g * REP + r + 1, :]
            qa_ref[g, r * C:(r + 1) * C, 0:LANES] = qg[g][r * C:(r + 1) * C]
            qa_ref[g, r * C:(r + 1) * C, LANES:2 * LANES] = (selb + feat).astype(BF16)

    m_ref[...] = jnp.full(m_ref.shape, M_INIT, F32)
    l_ref[...] = jnp.zeros(l_ref.shape, F32)
    acc_ref[...] = jnp.zeros(acc_ref.shape, F32)

    def key_tile(kt, causal):
        k0 = pl.multiple_of(kt * TK, TK)
        kk = jnp.concatenate([ksl_ref[pl.ds(k0, TK), :], kaug_ref[pl.ds(k0, TK), :]], axis=1)
        vv = vsl_ref[pl.ds(k0, TK), :]
        if causal:
            pos = k0 + lax.broadcasted_iota(jnp.int32, (C, TK), 1)
            tq = t0 + lax.broadcasted_iota(jnp.int32, (C, TK), 0)
            cb = jnp.where(pos <= tq, 0.0, NEG_INF)
            cb = jnp.concatenate([cb] * REP, axis=0)
        for g in range(GROUPS):
            s = _dot_nt(qa_ref[g], kk)
            if causal:
                s = s + cb
            m_old = m_ref[g]
            m_new = jnp.maximum(m_old, jnp.max(s, axis=-1, keepdims=True))
            alpha = jnp.exp(m_old - m_new)
            p = jnp.exp(s - _rep_lanes(m_new, TK // LANES))
            l_ref[g] = alpha * l_ref[g] + jnp.sum(p, axis=-1, keepdims=True)
            acc_ref[g] = alpha * acc_ref[g] + _dot(p.astype(BF16), vv)
            m_ref[g] = m_new

    n_full = t0 // TK

    def loop_body(kt, carry):
        key_tile(kt, False)
        return carry

    lax.fori_loop(0, n_full, loop_body, 0)
    key_tile(n_full, True)

    NW = W + C
    start = pl.multiple_of(jnp.maximum(t0 - W, 0), C)
    kw = jnp.concatenate([kwn_ref[pl.ds(start, NW), :], kaug_ref[pl.ds(start, NW), :]], axis=1)
    vw = vwn_ref[pl.ds(start, NW), :]
    dist = (t0 + lax.broadcasted_iota(jnp.int32, (C, NW), 0)) - (start + lax.broadcasted_iota(jnp.int32, (C, NW), 1))
    wb = jnp.where((dist >= 0) & (dist < W), 0.0, NEG_INF)
    wb = jnp.concatenate([wb] * REP, axis=0)
    o_win = []
    for g in range(GROUPS):
        feats = jnp.concatenate(
            [jnp.broadcast_to(qfeat_ref[g * REP + r:g * REP + r + 1, :], (C, LANES)) for r in range(REP)], axis=0)
        qw = jnp.concatenate([qg[g], feats.astype(BF16)], axis=1)
        s = _dot_nt(qw, kw) + wb
        e = jnp.exp(s - jnp.max(s, axis=-1, keepdims=True))
        p = e * (1.0 / jnp.sum(e, axis=-1, keepdims=True))
        o_win.append(_dot(p.astype(BF16), vw))

    gates = gates_ref[...]
    lane = lax.broadcasted_iota(jnp.int32, (C, LANES), 1)
    for r in range(REP):
        chunk = None
        for g in range(GROUPS):
            hcol = 3 * (g * REP + r)
            rows = slice(r * C, (r + 1) * C)
            o_sel = acc_ref[g, rows, :] * (1.0 / l_ref[g, rows, :])
            o = (gates[:, hcol:hcol + 1] * o_cmp[g][rows]
                 + gates[:, hcol + 1:hcol + 2] * o_sel
                 + gates[:, hcol + 2:hcol + 3] * o_win[g][rows])
            chunk = o if g == 0 else jnp.where(lane < HEAD_DIM, chunk, o)
        out_ref[:, r * LANES:(r + 1) * LANES] = chunk.astype(BF16)


def _nsa(slopes, qn, gates, kcmp, vcmp, nkv, kaug, qfeat, ovt, *, C, TK):
    B, S, _ = qn.shape
    NC = kcmp.shape[1]
    NSB = S // SEL_LEN
    n_top = min(N_SEL, NSB)
    W = NSA_WINDOW
    body = functools.partial(_nsa_body, C=C, TK=TK, W=W, NSB=NSB, NC=NC, n_top=n_top)
    qblk = lambda n: pl.BlockSpec((None, C, n), lambda b, i: (b, i, 0))
    per_b = lambda n: pl.BlockSpec((None, n, 128), lambda b, i: (b, 0, 0))
    nkv_col = lambda j: pl.BlockSpec((None, S, 128), lambda b, i, j=j: (b, 0, j))
    full = lambda a: pl.BlockSpec(a.shape, lambda b, i: (0,) * a.ndim)
    return pl.pallas_call(
        body,
        grid=(B, S // C),
        in_specs=[pl.BlockSpec(memory_space=pltpu.SMEM),
                  qblk(512), qblk(128), per_b(NC), per_b(NC),
                  nkv_col(0), nkv_col(1), nkv_col(2), nkv_col(3),
                  full(kaug), full(qfeat), full(ovt)],
        out_specs=qblk(512),
        out_shape=jax.ShapeDtypeStruct((B, S, 512), BF16),
        scratch_shapes=[
            pltpu.VMEM((GROUPS, REP * C, 2 * LANES), BF16),
            pltpu.VMEM((GROUPS, REP * C, LANES), F32),
            pltpu.VMEM((GROUPS, REP * C, LANES), F32),
            pltpu.VMEM((GROUPS, REP * C, LANES), F32),
            pltpu.VMEM((NSB, C), F32),
        ],
        compiler_params=_params(("parallel", "parallel")),
        name="nsa",
    )(slopes, qn, gates, kcmp, vcmp, nkv, nkv, nkv, nkv, kaug, qfeat, ovt)


def _swa_body(slopes_ref, sinks_ref, q_ref, kp_ref, kc_ref, vp_ref, vc_ref, out_ref, *, C):
    i = pl.program_id(1)
    q_all = q_ref[...]
    kk = jnp.concatenate([kp_ref[...], kc_ref[...]], axis=0)
    vv = jnp.concatenate([vp_ref[...], vc_ref[...]], axis=0)
    qi = lax.broadcasted_iota(jnp.int32, (C, 2 * C), 0)
    kj = lax.broadcasted_iota(jnp.int32, (C, 2 * C), 1)
    dist = C + qi - kj
    valid = (dist >= 0) & (dist < SWA_WINDOW) & ((i - 1) * C + kj >= 0)
    distf = dist.astype(F32)
    lane = lax.broadcasted_iota(jnp.int32, (C, LANES), 1)
    outs = []
    for g in range(GROUPS):
        s_all = _dot_nt(_group_rows(q_all, g, C), kk)
        probs = []
        for r in range(REP):
            h = g * REP + r
            s = jnp.where(valid, s_all[r * C:(r + 1) * C] - slopes_ref[h] * distf, NEG_INF)
            sink = sinks_ref[h]
            m = jnp.maximum(jnp.max(s, axis=-1, keepdims=True), sink)
            e = jnp.exp(s - m)
            den = jnp.sum(e, axis=-1, keepdims=True) + jnp.exp(sink - m)
            probs.append((e * (1.0 / den)).astype(BF16))
        outs.append(_dot(jnp.concatenate(probs, axis=0), vv))
    for r in range(REP):
        rows = slice(r * C, (r + 1) * C)
        chunk = jnp.where(lane < HEAD_DIM, outs[0][rows], outs[1][rows])
        out_ref[:, r * LANES:(r + 1) * LANES] = chunk.astype(BF16)


def _swa(slopes, sinks, swa_qkv, *, C):
    B, S, _ = swa_qkv.shape
    body = functools.partial(_swa_body, C=C)
    smem = pl.BlockSpec(memory_space=pltpu.SMEM)
    q_spec = pl.BlockSpec((None, C, 512), lambda b, i: (b, i, 0))
    cur = lambda j: pl.BlockSpec((None, C, 128), lambda b, i, j=j: (b, i, j))
    prev = lambda j: pl.BlockSpec((None, C, 128), lambda b, i, j=j: (b, jnp.maximum(i - 1, 0), j))
    return pl.pallas_call(
        body,
        grid=(B, S // C),
        in_specs=[smem, smem, q_spec, prev(4), cur(4), prev(5), cur(5)],
        out_specs=pl.BlockSpec((None, C, 512), lambda b, i: (b, i, 0)),
        out_shape=jax.ShapeDtypeStruct((B, S, 512), BF16),
        compiler_params=_params(("parallel", "parallel")),
        name="swa",
    )(slopes, sinks, swa_qkv, swa_qkv, swa_qkv, swa_qkv, swa_qkv)


def _outproj_body(on_ref, os_ref, x_ref, wn_ref, ws_ref, g_ref, x1_ref):
    mix = _dot(on_ref[...], wn_ref[...]) + _dot(os_ref[...], ws_ref[...])
    x1_ref[...] = x_ref[...] + _rms(mix, g_ref[...])


def _outproj(o_nsa, o_swa, x2d, wn, ws, g, tm):
    T = x2d.shape[0]
    row = lambda n: pl.BlockSpec((tm, n), lambda i: (i, 0))
    full = lambda a: pl.BlockSpec(a.shape, lambda i: (0,) * a.ndim)
    return pl.pallas_call(
        _outproj_body,
        grid=(T // tm,),
        in_specs=[row(512), row(512), row(D_MODEL), full(wn), full(ws), full(g)],
        out_specs=row(D_MODEL),
        out_shape=jax.ShapeDtypeStruct((T, D_MODEL), F32),
        compiler_params=_params(("parallel",)),
        name="outproj",
    )(o_nsa, o_swa, x2d, wn, ws, g)


HALO = 8
FFN_CHUNK = 256


def _ffn_body(xh_ref, x_ref, p_ref, g1_ref, wg_ref, wu_ref, cw_ref, cb_ref, wd_ref, g2_ref,
              wple_ref, wpg_ref, out_ref, act_ref, *, tm, tiles_per_seq):
    first = (pl.program_id(0) % tiles_per_seq) == 0
    xh = jnp.where(first, 0.0, xh_ref[...])
    x = x_ref[...]
    g1 = g1_ref[...]
    he = jnp.concatenate([_rms(xh, g1), _rms(x, g1)], axis=0).astype(BF16)
    h = he[HALO:]
    for c in range(D_FF // FFN_CHUNK):
        cols = slice(c * FFN_CHUNK, (c + 1) * FFN_CHUNK)
        a = _dot(he, wg_ref[:, cols])
        u = _dot(h, wu_ref[:, cols])
        conv = (cw_ref[2:3, cols] * a[HALO:]
                + cw_ref[1:2, cols] * pltpu.roll(a, 1, 0)[HALO:]
                + cw_ref[0:1, cols] * pltpu.roll(a, 2, 0)[HALO:]
                + cb_ref[:, cols])
        act_ref[:, cols] = (_gelu_tanh(conv) * u).astype(BF16)
    y = _dot(act_ref[...], wd_ref[...])
    x2 = x + _rms(y, g2_ref[...])
    e = _dot(p_ref[...].astype(BF16), wple_ref[...])
    gate = jax.nn.sigmoid(_dot(x2.astype(BF16), wpg_ref[...]))
    out_ref[...] = x2 + e * gate


def _ffn(x1, p2d, g1, wg, wu, cw, cb, wd, g2, wple, wpg, tm, S):
    T = x1.shape[0]
    body = functools.partial(_ffn_body, tm=tm, tiles_per_seq=S // tm)
    row = lambda n: pl.BlockSpec((tm, n), lambda i: (i, 0))
    halo = pl.BlockSpec((HALO, D_MODEL), lambda i: (jnp.maximum(i * (tm // HALO) - 1, 0), 0))
    const = lambda a: pl.BlockSpec(a.shape, lambda i: (0,) * a.ndim, pipeline_mode=pl.Buffered(1))
    return pl.pallas_call(
        body,
        grid=(T // tm,),
        in_specs=[halo, row(D_MODEL), row(PLE_DIM), const(g1), const(wg), const(wu), const(cw), const(cb),
                  const(wd), const(g2), const(wple), const(wpg)],
        out_specs=row(D_MODEL),
        out_shape=jax.ShapeDtypeStruct((T, D_MODEL), F32),
        scratch_shapes=[pltpu.VMEM((tm, D_FF), BF16)],
        compiler_params=_params(("parallel",)),
        name="ffn",
    )(x1, x1, p2d, g1, wg, wu, cw, cb, wd, g2, wple, wpg)


def _alibi_slopes():
    h = jnp.arange(N_HEADS, dtype=F32)
    return jnp.exp2(-8.0 * (h + 1.0) / N_HEADS)


def _key_features(S):
    pos = np.arange(S)
    blk, off = pos // SEL_LEN, pos % SEL_LEN
    feat = np.zeros((S, LANES), np.float32)
    feat[pos, blk] = 1.0
    base = S // SEL_LEN
    assert base <= HEAD_DIM
    for k in range(N_ALIBI_TERMS):
        feat[:, HEAD_DIM + k] = blk * SEL_LEN
        feat[:, HEAD_DIM + N_ALIBI_TERMS + k] = off
    return jnp.asarray(feat, BF16)


def _query_features(slopes):
    terms, rest = [], slopes
    for _ in range(N_ALIBI_TERMS):
        t = rest.astype(BF16).astype(F32)
        terms.append(t)
        rest = rest - t
    tt = jnp.stack(terms, axis=1)
    feat = jnp.zeros((slopes.shape[0], LANES), F32)
    feat = feat.at[:, HEAD_DIM:HEAD_DIM + N_ALIBI_TERMS].set(tt)
    feat = feat.at[:, HEAD_DIM + N_ALIBI_TERMS:HEAD_DIM + 2 * N_ALIBI_TERMS].set(tt)
    return feat


def _overlap_t(S, NC):
    n = np.arange(NC)[None, :]
    j = np.arange(S // SEL_LEN)[:, None]
    cs, ss = n * CMP_STRIDE, j * SEL_LEN
    ov = (cs < ss + SEL_LEN) & (cs + CMP_LEN - 1 >= ss)
    return jnp.asarray(ov.astype(np.float32), BF16)


def _head_cols(perm):
    return np.concatenate([np.arange(h * HEAD_DIM, (h + 1) * HEAD_DIM) for h in perm])


def kernel(x, p, attn_pre_g, w_in, cmp_pe_k, cmp_w1_k, cmp_w2_k, cmp_pe_v, cmp_w1_v, cmp_w2_v, sinks, w_o,
           attn_post_g, mlp_pre_g, w_gate_up, conv_w, conv_b, w_down, mlp_post_g, w_ple, w_ple_gate):
    B, S, D = x.shape
    T = B * S
    depth = w_in.shape[0]
    slopes = _alibi_slopes()
    swa_slopes, nsa_slopes = slopes[:SWA_HEADS], slopes[SWA_HEADS:]
    NS = S // CMP_STRIDE
    kaug = _key_features(S)
    qfeat = _query_features(nsa_slopes)
    ovt = _overlap_t(S, NS)
    perm = _head_cols(HEAD_PERM)
    scale = HEAD_DIM ** -0.5
    tm = 512

    xf = x.reshape(T, D)
    for i in range(depth):
        w = w_in[i]
        q_n = w[:, 0:512][:, perm] * scale
        q_s = w[:, 1304:1816][:, perm] * scale
        gate_cols = jnp.pad(w[:, 1280:1304], ((0, 0), (0, LANES - 24)))
        w_cat = jnp.concatenate([q_n, w[:, 512:1280], q_s, w[:, 1816:2072], gate_cols], axis=1).astype(BF16)
        qn, kc, vc, nkv, swa_qkv, gates = _inproj(xf, attn_pre_g[i][None, :], w_cat, tm)

        wk = _compress_weights(cmp_pe_k[i], cmp_w1_k[i], cmp_w2_k[i])
        wv = _compress_weights(cmp_pe_v[i], cmp_w1_v[i], cmp_w2_v[i])
        seg_w = CMP_STRIDE * GROUPS * HEAD_DIM
        kcmp, vcmp = _compress(kc.reshape(B, NS, seg_w), vc.reshape(B, NS, seg_w), wk, wv)

        o_nsa = _nsa(nsa_slopes, qn.reshape(B, S, 512), gates.reshape(B, S, 128), kcmp, vcmp,
                     nkv.reshape(B, S, 512), kaug, qfeat, ovt, C=128, TK=512)
        o_swa = _swa(swa_slopes, sinks[i], swa_qkv.reshape(B, S, 768), C=128)

        wo = w_o[i]
        wo_n = wo[:512][perm].astype(BF16)
        wo_s = wo[512:][perm].astype(BF16)
        x1 = _outproj(o_nsa.reshape(T, 512), o_swa.reshape(T, 512), xf, wo_n, wo_s, attn_post_g[i][None, :], tm)

        wgu = w_gate_up[i].astype(BF16)
        xf = _ffn(x1, p[i].reshape(T, PLE_DIM), mlp_pre_g[i][None, :], wgu[:, :D_FF], wgu[:, D_FF:],
                  conv_w[i], conv_b[i][None, :], w_down[i].astype(BF16), mlp_post_g[i][None, :],
                  w_ple[i].astype(BF16), w_ple_gate[i].astype(BF16), tm, S)
    return xf.reshape(B, S, D)
```

```python
import functools

import numpy as np
import jax
import jax.numpy as jnp
from jax import lax
from jax.experimental import pallas as pl
from jax.experimental.pallas import tpu as pltpu

F32 = jnp.float32
BF16 = jnp.bfloat16

D_MODEL = 1024
HEAD_DIM = 64
NSA_HEADS = 8
SWA_HEADS = 8
N_HEADS = NSA_HEADS + SWA_HEADS
CMP_LEN = 32
CMP_STRIDE = 16
CMP_HIDDEN = 128
SEL_LEN = 64
N_SEL = 16
NSA_WINDOW = 512
SWA_WINDOW = 128
D_FF = 2816
CONV_W = 3
PLE_DIM = 256
RMS_EPS = 1e-6
NEG_INF = -1e30
FORCE_BONUS = 1e6

LANES = 128
MASK_BIG = 2.0 ** 100
M_INIT = -3.0e38
GROUPS = 2
REP = 4
HEAD_PERM = (0, 4, 1, 5, 2, 6, 3, 7)
N_ALIBI_TERMS = 3
LOG2E = 1.4426950408889634

VMEM_LIMIT = 56 * 1024 * 1024


def _rms(x, g):
    return x * lax.rsqrt(jnp.mean(x * x, axis=-1, keepdims=True) + RMS_EPS) * g


def _gelu_tanh(x):
    c = np.sqrt(2.0 / np.pi).astype(np.float32)
    return x * (0.5 * (1.0 + jnp.tanh(c * (x + 0.044715 * (x * x * x)))))


def _dot(a, b):
    return jnp.dot(a, b, preferred_element_type=F32)


def _dot_nt(a, b):
    return lax.dot_general(a, b, (((1,), (1,)), ((), ())), preferred_element_type=F32)


def _rep_lanes(a, n):
    return a if n == 1 else jnp.concatenate([a] * n, axis=1)


def _params(sem):
    return pltpu.CompilerParams(dimension_semantics=sem, vmem_limit_bytes=VMEM_LIMIT)


def _inproj_body(x_ref, g_ref, w_ref, qn_ref, kc_ref, vc_ref, nkv_ref, swa_ref, gates_ref):
    h = _rms(x_ref[...], g_ref[...]).astype(BF16)
    proj = _dot(h, w_ref[...])
    qn_ref[...] = proj[:, 0:512].astype(BF16)
    kc_ref[...] = proj[:, 512:640].astype(BF16)
    vc_ref[...] = proj[:, 640:768].astype(BF16)
    nkv_ref[...] = proj[:, 768:1280].astype(BF16)
    swa_ref[...] = proj[:, 1280:2048].astype(BF16)
    gates_ref[...] = jax.nn.sigmoid(proj[:, 2048:2176])


def _inproj(x2d, g, w_cat, tm):
    T = x2d.shape[0]
    wn = w_cat.shape[1]
    row = lambda n: pl.BlockSpec((tm, n), lambda i: (i, 0))
    full = lambda a: pl.BlockSpec(a.shape, lambda i: (0,) * a.ndim)
    return pl.pallas_call(
        _inproj_body,
        grid=(T // tm,),
        in_specs=[row(D_MODEL), full(g), full(w_cat)],
        out_specs=[row(512), row(128), row(128), row(512), row(768), row(128)],
        out_shape=[
            jax.ShapeDtypeStruct((T, 512), BF16),
            jax.ShapeDtypeStruct((T, 128), BF16),
            jax.ShapeDtypeStruct((T, 128), BF16),
            jax.ShapeDtypeStruct((T, 512), BF16),
            jax.ShapeDtypeStruct((T, 768), BF16),
            jax.ShapeDtypeStruct((T, 128), F32),
        ],
        compiler_params=_params(("parallel",)),
        name="inproj",
    )(x2d, g, w_cat)


def _compress_body(kc_ref, vc_ref,
                   pea_k, peb_k, w1a_k, w1b_k, w2_k,
                   pea_v, peb_v, w1a_v, w1b_v, w2_v,
                   kcmp_ref, vcmp_ref):
    def one(seg_ref, pea, peb, w1a, w1b, w2, out_ref):
        seg = seg_ref[...].astype(F32)
        ns = seg.shape[0]
        ha = _dot((seg + pea[...]).astype(BF16), w1a[...])
        hb = _dot((seg + peb[...]).astype(BF16), w1b[...])
        hid = ha + pltpu.roll(hb, ns - 1, 0)
        act = _gelu_tanh(hid).astype(BF16)
        out_ref[...] = _dot(act, w2[...]).astype(BF16)

    one(kc_ref, pea_k, peb_k, w1a_k, w1b_k, w2_k, kcmp_ref)
    one(vc_ref, pea_v, peb_v, w1a_v, w1b_v, w2_v, vcmp_ref)


def _compress_weights(pe, w1, w2):
    half = CMP_LEN // 2
    eye = jnp.eye(GROUPS, dtype=F32)
    w1r = w1.reshape(CMP_LEN, HEAD_DIM, CMP_HIDDEN)

    def expand(w):
        return jnp.einsum("ldj,pg->lpdgj", w, eye).reshape(half * GROUPS * HEAD_DIM, GROUPS * CMP_HIDDEN)

    def pe_row(p):
        return jnp.broadcast_to(p[:, None, :], (half, GROUPS, HEAD_DIM)).reshape(1, half * GROUPS * HEAD_DIM)

    w2e = jnp.einsum("jd,pg->pjgd", w2, eye).reshape(GROUPS * CMP_HIDDEN, GROUPS * HEAD_DIM)
    return (pe_row(pe[:half]), pe_row(pe[half:]),
            expand(w1r[:half]).astype(BF16), expand(w1r[half:]).astype(BF16), w2e.astype(BF16))


def _compress(kc_seg, vc_seg, wk, wv):
    B, NS, WD = kc_seg.shape
    seg = pl.BlockSpec((None, NS, WD), lambda b: (b, 0, 0))
    full = lambda a: pl.BlockSpec(a.shape, lambda b: (0,) * a.ndim)
    out = pl.BlockSpec((None, NS, 128), lambda b: (b, 0, 0))
    return pl.pallas_call(
        _compress_body,
        grid=(B,),
        in_specs=[seg, seg] + [full(a) for a in wk] + [full(a) for a in wv],
        out_specs=[out, out],
        out_shape=[jax.ShapeDtypeStruct((B, NS, 128), BF16)] * 2,
        compiler_params=_params(("parallel",)),
        name="compress",
    )(kc_seg, vc_seg, *wk, *wv)


def _group_rows(q_all, g, C):
    lane = lax.broadcasted_iota(jnp.int32, (C, LANES), 1)
    keep = (lane < HEAD_DIM) if g == 0 else (lane >= HEAD_DIM)
    zero = jnp.zeros((C, LANES), q_all.dtype)
    return jnp.concatenate(
        [jnp.where(keep, q_all[:, r * LANES:(r + 1) * LANES], zero) for r in range(REP)], axis=0)


def _nsa_body(slopes_ref, qn_ref, gates_ref, kcmp_ref, vcmp_ref, ksl_ref, vsl_ref, kwn_ref, vwn_ref,
              kaug_ref, qfeat_ref, ovt_ref, out_ref,
              qa_ref, m_ref, l_ref, acc_ref, part_ref, imp_ref, *, C, TK, W, NSB, NC, n_top):
    t0 = pl.program_id(1) * C
    q_all = qn_ref[...]
    qg = [_group_rows(q_all, g, C) for g in range(GROUPS)]
    gates = gates_ref[...]

    def softmax_rows(s):
        e = jnp.exp2(s - jnp.max(s, axis=-1, keepdims=True))
        return e * (1.0 / jnp.sum(e, axis=-1, keepdims=True))

    NW = W + C
    start = pl.multiple_of(jnp.maximum(t0 - W, 0), C)
    kw = jnp.concatenate([kwn_ref[pl.ds(start, NW), :], kaug_ref[pl.ds(start, NW), :]], axis=1)
    vw = vwn_ref[pl.ds(start, NW), :]
    dist = (t0 + lax.broadcasted_iota(jnp.int32, (C, NW), 0)) - (start + lax.broadcasted_iota(jnp.int32, (C, NW), 1))
    wb = jnp.where((dist >= 0) & (dist < W), 0.0, NEG_INF)
    for g in range(GROUPS):
        feats = jnp.concatenate(
            [jnp.broadcast_to(qfeat_ref[g * REP + r:g * REP + r + 1, :], (C, LANES)) for r in range(REP)], axis=0)
        qw = jnp.concatenate([qg[g], feats.astype(BF16)], axis=1)
        s = _dot_nt(qw, kw)
        probs = [softmax_rows(s[r * C:(r + 1) * C] + wb).astype(BF16) for r in range(REP)]
        o_win = _dot(jnp.concatenate(probs, axis=0), vw)
        for r in range(REP):
            hcol = 3 * (g * REP + r)
            part_ref[g, r * C:(r + 1) * C, :] = gates[:, hcol + 2:hcol + 3] * o_win[r * C:(r + 1) * C]

    tq_c = t0 + lax.broadcasted_iota(jnp.int32, (C, NC), 0)
    cend = lax.broadcasted_iota(jnp.int32, (C, NC), 1) * CMP_STRIDE + (CMP_LEN - 1)
    valid_c = cend <= tq_c
    cend_rel = (cend - t0).astype(F32)
    for g in range(GROUPS):
        sc = _dot_nt(qg[g], kcmp_ref[...])
        psum = jnp.zeros((C, NC), F32)
        probs = []
        for r in range(REP):
            s = sc[r * C:(r + 1) * C] + slopes_ref[g * REP + r] * cend_rel
            p = jnp.where(valid_c, softmax_rows(jnp.where(valid_c, s, NEG_INF)), 0.0)
            psum = psum + p
            probs.append(p.astype(BF16))
        o_cmp = _dot(jnp.concatenate(probs, axis=0), vcmp_ref[...])
        for r in range(REP):
            hcol = 3 * (g * REP + r)
            part_ref[g, r * C:(r + 1) * C, :] += gates[:, hcol:hcol + 1] * o_cmp[r * C:(r + 1) * C]

        ovt = ovt_ref[...]
        p_hi = psum.astype(BF16)
        r1 = psum - p_hi.astype(F32)
        p_mid = r1.astype(BF16)
        p_lo = (r1 - p_mid.astype(F32)).astype(BF16)
        imp = _dot_nt(ovt, p_hi) + _dot_nt(ovt, p_mid) + _dot_nt(ovt, p_lo)

        jblk = lax.broadcasted_iota(jnp.int32, (NSB, C), 0)
        jt = lax.shift_right_logical(t0 + lax.broadcasted_iota(jnp.int32, (NSB, C), 1), 6)
        forced = (jblk == 0) | (jblk == jt) | (jblk == jt - 1)
        visible = jblk <= jt
        imp = jnp.where(visible, jnp.where(forced, FORCE_BONUS, imp), NEG_INF)
        imp_ref[...] = imp
        nv = NSB // 8
        jrow = lax.broadcasted_iota(jnp.int32, (8, LANES), 0)
        selb_chunks = []
        for c0 in range(0, C, LANES):
            cols = slice(c0, c0 + LANES)
            blks = [imp_ref[8 * v:8 * v + 8, cols] for v in range(nv)]
            cnt = [jnp.zeros((8, LANES), F32) for _ in range(nv)]
            for i in range(NSB):
                row = jnp.broadcast_to(imp_ref[i:i + 1, cols], (8, LANES))
                for v in range(nv):
                    if i < 8 * v:
                        before = row >= blks[v]
                    elif i >= 8 * v + 8:
                        before = row > blks[v]
                    else:
                        before = (row > blks[v]) | ((row == blks[v]) & (jrow > i - 8 * v))
                    cnt[v] = cnt[v] + jnp.where(before, 1.0, 0.0)
            rank = jnp.concatenate(cnt, axis=0)
            visible_c = (lax.broadcasted_iota(jnp.int32, (NSB, LANES), 0)
                         <= lax.shift_right_logical(t0 + c0 + lax.broadcasted_iota(jnp.int32, (NSB, LANES), 1), 6))
            sel_t = jnp.where(visible_c & (rank < n_top), 0.0, -MASK_BIG)
            sel_t = jnp.concatenate([sel_t, jnp.zeros((LANES - NSB, LANES), F32)], axis=0)
            selb_chunks.append(sel_t.T)
        selb = jnp.concatenate(selb_chunks, axis=0)
        for r in range(REP):
            feat = qfeat_ref[g * REP + r:g * REP + r + 1, :]
            qa_ref[g, r * C:(r + 1) * C, 0:LANES] = qg[g][r * C:(r + 1) * C]
            qa_ref[g, r * C:(r + 1) * C, LANES:2 * LANES] = (selb + feat).astype(BF16)

    m_ref[...] = jnp.full(m_ref.shape, M_INIT, F32)
    l_ref[...] = jnp.zeros(l_ref.shape, F32)
    acc_ref[...] = jnp.zeros(acc_ref.shape, F32)
    tq = t0 + lax.broadcasted_iota(jnp.int32, (C, TK), 0)
    kcol = lax.broadcasted_iota(jnp.int32, (C, TK), 1)

    def key_tile(kt):
        k0 = pl.multiple_of(kt * TK, TK)
        kk = jnp.concatenate([ksl_ref[pl.ds(k0, TK), :], kaug_ref[pl.ds(k0, TK), :]], axis=1)
        vv = vsl_ref[pl.ds(k0, TK), :]
        cb = jnp.where(k0 + kcol <= tq, 0.0, NEG_INF)
        for g in range(GROUPS):
            s = _dot_nt(qa_ref[g], kk)
            probs = []
            for r in range(REP):
                rows = slice(r * C, (r + 1) * C)
                sr = s[rows] + cb
                m_old = m_ref[g, rows, :]
                m_new = jnp.maximum(m_old, jnp.max(sr, axis=-1, keepdims=True))
                alpha = jnp.exp2(m_old - m_new)
                p = jnp.exp2(sr - _rep_lanes(m_new, TK // LANES))
                l_ref[g, rows, :] = alpha * l_ref[g, rows, :] + jnp.sum(p, axis=-1, keepdims=True)
                acc_ref[g, rows, :] = alpha * acc_ref[g, rows, :]
                m_ref[g, rows, :] = m_new
                probs.append(p.astype(BF16))
            acc_ref[g] += _dot(jnp.concatenate(probs, axis=0), vv)

    n_tiles = t0 // TK + 1

    def pair_body(k, carry):
        key_tile(2 * k)
        key_tile(2 * k + 1)
        return carry

    lax.fori_loop(0, n_tiles // 2, pair_body, 0)

    @pl.when(n_tiles % 2 == 1)
    def _():
        key_tile(n_tiles - 1)

    lane = lax.broadcasted_iota(jnp.int32, (C, LANES), 1)
    for r in range(REP):
        chunk = None
        for g in range(GROUPS):
            hcol = 3 * (g * REP + r)
            rows = slice(r * C, (r + 1) * C)
            o_sel = acc_ref[g, rows, :] * (1.0 / l_ref[g, rows, :])
            o = part_ref[g, rows, :] + gates[:, hcol + 1:hcol + 2] * o_sel
            chunk = o if g == 0 else jnp.where(lane < HEAD_DIM, chunk, o)
        out_ref[:, r * LANES:(r + 1) * LANES] = chunk.astype(BF16)


def _nsa(slopes, qn, gates, kcmp, vcmp, nkv, kaug, qfeat, ovt, *, C, TK):
    B, S, _ = qn.shape
    NC = kcmp.shape[1]
    NSB = S // SEL_LEN
    n_top = min(N_SEL, NSB)
    W = NSA_WINDOW
    body = functools.partial(_nsa_body, C=C, TK=TK, W=W, NSB=NSB, NC=NC, n_top=n_top)
    qblk = lambda n: pl.BlockSpec((None, C, n), lambda b, i: (b, i, 0))
    per_b = lambda n: pl.BlockSpec((None, n, 128), lambda b, i: (b, 0, 0))
    nkv_col = lambda j: pl.BlockSpec((None, S, 128), lambda b, i, j=j: (b, 0, j))
    full = lambda a: pl.BlockSpec(a.shape, lambda b, i: (0,) * a.ndim)
    return pl.pallas_call(
        body,
        grid=(B, S // C),
        in_specs=[pl.BlockSpec(memory_space=pltpu.SMEM),
                  qblk(512), qblk(128), per_b(NC), per_b(NC),
                  nkv_col(0), nkv_col(1), nkv_col(2), nkv_col(3),
                  full(kaug), full(qfeat), full(ovt)],
        out_specs=qblk(512),
        out_shape=jax.ShapeDtypeStruct((B, S, 512), BF16),
        scratch_shapes=[
            pltpu.VMEM((GROUPS, REP * C, 2 * LANES), BF16),
            pltpu.VMEM((GROUPS, REP * C, LANES), F32),
            pltpu.VMEM((GROUPS, REP * C, LANES), F32),
            pltpu.VMEM((GROUPS, REP * C, LANES), F32),
            pltpu.VMEM((GROUPS, REP * C, LANES), F32),
            pltpu.VMEM((NSB, C), F32),
        ],
        compiler_params=_params(("parallel", "parallel")),
        name="nsa",
    )(slopes, qn, gates, kcmp, vcmp, nkv, nkv, nkv, nkv, kaug, qfeat, ovt)


def _swa_body(sinks_ref, q_ref, kp_ref, kc_ref, vp_ref, vc_ref, bias_ref, out_ref, *, C):
    first = jnp.where(pl.program_id(1) == 0, 1, 0)
    q_all = q_ref[...]
    kk = jnp.concatenate([kp_ref[...], kc_ref[...]], axis=0)
    vv = jnp.concatenate([vp_ref[...], vc_ref[...]], axis=0)
    lane = lax.broadcasted_iota(jnp.int32, (C, LANES), 1)
    outs = []
    for g in range(GROUPS):
        s_all = _dot_nt(_group_rows(q_all, g, C), kk)
        probs = []
        for r in range(REP):
            h = g * REP + r
            s = s_all[r * C:(r + 1) * C] + bias_ref[first, h]
            sink = sinks_ref[h]
            m = jnp.maximum(jnp.max(s, axis=-1, keepdims=True), sink)
            e = jnp.exp2(s - m)
            den = jnp.sum(e, axis=-1, keepdims=True) + jnp.exp2(sink - m)
            probs.append((e * (1.0 / den)).astype(BF16))
        outs.append(_dot(jnp.concatenate(probs, axis=0), vv))
    for r in range(REP):
        rows = slice(r * C, (r + 1) * C)
        chunk = jnp.where(lane < HEAD_DIM, outs[0][rows], outs[1][rows])
        out_ref[:, r * LANES:(r + 1) * LANES] = chunk.astype(BF16)


def _swa_bias(slopes_l2, C):
    qi = np.arange(C)[:, None]
    kj = np.arange(2 * C)[None, :]
    dist = C + qi - kj
    valid = (dist >= 0) & (dist < SWA_WINDOW)
    tab = -slopes_l2[:, None, None] * jnp.asarray(dist, F32)[None]
    return jnp.stack([jnp.where(valid[None], tab, NEG_INF), jnp.where((valid & (kj >= C))[None], tab, NEG_INF)])


def _swa(sinks_l2, swa_qkv, bias, *, C):
    B, S, _ = swa_qkv.shape
    body = functools.partial(_swa_body, C=C)
    smem = pl.BlockSpec(memory_space=pltpu.SMEM)
    q_spec = pl.BlockSpec((None, C, 512), lambda b, i: (b, i, 0))
    cur = lambda j: pl.BlockSpec((None, C, 128), lambda b, i, j=j: (b, i, j))
    prev = lambda j: pl.BlockSpec((None, C, 128), lambda b, i, j=j: (b, jnp.maximum(i - 1, 0), j))
    return pl.pallas_call(
        body,
        grid=(B, S // C),
        in_specs=[smem, q_spec, prev(4), cur(4), prev(5), cur(5),
                  pl.BlockSpec(bias.shape, lambda b, i: (0, 0, 0, 0))],
        out_specs=pl.BlockSpec((None, C, 512), lambda b, i: (b, i, 0)),
        out_shape=jax.ShapeDtypeStruct((B, S, 512), BF16),
        compiler_params=_params(("parallel", "parallel")),
        name="swa",
    )(sinks_l2, swa_qkv, swa_qkv, swa_qkv, swa_qkv, swa_qkv, bias)


def _outproj_body(on_ref, os_ref, x_ref, wn_ref, ws_ref, g_ref, x1_ref):
    mix = _dot(on_ref[...], wn_ref[...]) + _dot(os_ref[...], ws_ref[...])
    x1_ref[...] = x_ref[...] + _rms(mix, g_ref[...])


def _outproj(o_nsa, o_swa, x2d, wn, ws, g, tm):
    T = x2d.shape[0]
    row = lambda n: pl.BlockSpec((tm, n), lambda i: (i, 0))
    full = lambda a: pl.BlockSpec(a.shape, lambda i: (0,) * a.ndim)
    return pl.pallas_call(
        _outproj_body,
        grid=(T // tm,),
        in_specs=[row(512), row(512), row(D_MODEL), full(wn), full(ws), full(g)],
        out_specs=row(D_MODEL),
        out_shape=jax.ShapeDtypeStruct((T, D_MODEL), F32),
        compiler_params=_params(("parallel",)),
        name="outproj",
    )(o_nsa, o_swa, x2d, wn, ws, g)


HALO = 8
FFN_CHUNK = 256


def _ffn_body(xh_ref, x_ref, p_ref, g1_ref, wg_ref, wu_ref, cw_ref, cb_ref, wd_ref, g2_ref,
              wple_ref, wpg_ref, out_ref, act_ref, *, tm, tiles_per_seq):
    first = (pl.program_id(0) % tiles_per_seq) == 0
    xh = jnp.where(first, 0.0, xh_ref[...])
    x = x_ref[...]
    g1 = g1_ref[...]
    he = jnp.concatenate([_rms(xh, g1), _rms(x, g1)], axis=0).astype(BF16)
    h = he[HALO:]
    for c in range(D_FF // FFN_CHUNK):
        cols = slice(c * FFN_CHUNK, (c + 1) * FFN_CHUNK)
        a = _dot(he, wg_ref[:, cols])
        u = _dot(h, wu_ref[:, cols])
        conv = (cw_ref[2:3, cols] * a[HALO:]
                + cw_ref[1:2, cols] * pltpu.roll(a, 1, 0)[HALO:]
                + cw_ref[0:1, cols] * pltpu.roll(a, 2, 0)[HALO:]
                + cb_ref[:, cols])
        act_ref[:, cols] = (_gelu_tanh(conv) * u).astype(BF16)
    y = _dot(act_ref[...], wd_ref[...])
    x2 = x + _rms(y, g2_ref[...])
    e = _dot(p_ref[...].astype(BF16), wple_ref[...])
    gate = jax.nn.sigmoid(_dot(x2.astype(BF16), wpg_ref[...]))
    out_ref[...] = x2 + e * gate


def _ffn(x1, p2d, g1, wg, wu, cw, cb, wd, g2, wple, wpg, tm, S):
    T = x1.shape[0]
    body = functools.partial(_ffn_body, tm=tm, tiles_per_seq=S // tm)
    row = lambda n: pl.BlockSpec((tm, n), lambda i: (i, 0))
    halo = pl.BlockSpec((HALO, D_MODEL), lambda i: (jnp.maximum(i * (tm // HALO) - 1, 0), 0))
    const = lambda a: pl.BlockSpec(a.shape, lambda i: (0,) * a.ndim, pipeline_mode=pl.Buffered(1))
    return pl.pallas_call(
        body,
        grid=(T // tm,),
        in_specs=[halo, row(D_MODEL), row(PLE_DIM), const(g1), const(wg), const(wu), const(cw), const(cb),
                  const(wd), const(g2), const(wple), const(wpg)],
        out_specs=row(D_MODEL),
        out_shape=jax.ShapeDtypeStruct((T, D_MODEL), F32),
        scratch_shapes=[pltpu.VMEM((tm, D_FF), BF16)],
        compiler_params=_params(("parallel",)),
        name="ffn",
    )(x1, x1, p2d, g1, wg, wu, cw, cb, wd, g2, wple, wpg)


def _alibi_slopes():
    h = jnp.arange(N_HEADS, dtype=F32)
    return jnp.exp2(-8.0 * (h + 1.0) / N_HEADS)


def _key_features(S):
    pos = np.arange(S)
    blk, off = pos // SEL_LEN, pos % SEL_LEN
    feat = np.zeros((S, LANES), np.float32)
    feat[pos, blk] = 1.0
    base = S // SEL_LEN
    assert base <= HEAD_DIM
    for k in range(N_ALIBI_TERMS):
        feat[:, HEAD_DIM + k] = blk * SEL_LEN
        feat[:, HEAD_DIM + N_ALIBI_TERMS + k] = off
    return jnp.asarray(feat, BF16)


def _query_features(slopes):
    terms, rest = [], slopes
    for _ in range(N_ALIBI_TERMS):
        t = rest.astype(BF16).astype(F32)
        terms.append(t)
        rest = rest - t
    tt = jnp.stack(terms, axis=1)
    feat = jnp.zeros((slopes.shape[0], LANES), F32)
    feat = feat.at[:, HEAD_DIM:HEAD_DIM + N_ALIBI_TERMS].set(tt)
    feat = feat.at[:, HEAD_DIM + N_ALIBI_TERMS:HEAD_DIM + 2 * N_ALIBI_TERMS].set(tt)
    return feat


def _overlap_t(S, NC):
    n = np.arange(NC)[None, :]
    j = np.arange(S // SEL_LEN)[:, None]
    cs, ss = n * CMP_STRIDE, j * SEL_LEN
    ov = (cs < ss + SEL_LEN) & (cs + CMP_LEN - 1 >= ss)
    return jnp.asarray(ov.astype(np.float32), BF16)


def _head_cols(perm):
    return np.concatenate([np.arange(h * HEAD_DIM, (h + 1) * HEAD_DIM) for h in perm])


def kernel(x, p, attn_pre_g, w_in, cmp_pe_k, cmp_w1_k, cmp_w2_k, cmp_pe_v, cmp_w1_v, cmp_w2_v, sinks, w_o,
           attn_post_g, mlp_pre_g, w_gate_up, conv_w, conv_b, w_down, mlp_post_g, w_ple, w_ple_gate):
    B, S, D = x.shape
    T = B * S
    depth = w_in.shape[0]
    slopes = _alibi_slopes()
    swa_slopes, nsa_slopes = slopes[:SWA_HEADS] * LOG2E, slopes[SWA_HEADS:] * LOG2E
    NS = S // CMP_STRIDE
    kaug = _key_features(S)
    qfeat = _query_features(nsa_slopes)
    ovt = _overlap_t(S, NS)
    swa_bias = _swa_bias(swa_slopes, SWA_WINDOW)
    perm = _head_cols(HEAD_PERM)
    scale = HEAD_DIM ** -0.5 * LOG2E
    tm = 512

    xf = x.reshape(T, D)
    for i in range(depth):
        w = w_in[i]
        q_n = w[:, 0:512][:, perm] * scale
        q_s = w[:, 1304:1816][:, perm] * scale
        gate_cols = jnp.pad(w[:, 1280:1304], ((0, 0), (0, LANES - 24)))
        w_cat = jnp.concatenate([q_n, w[:, 512:1280], q_s, w[:, 1816:2072], gate_cols], axis=1).astype(BF16)
        qn, kc, vc, nkv, swa_qkv, gates = _inproj(xf, attn_pre_g[i][None, :], w_cat, tm)

        wk = _compress_weights(cmp_pe_k[i], cmp_w1_k[i], cmp_w2_k[i])
        wv = _compress_weights(cmp_pe_v[i], cmp_w1_v[i], cmp_w2_v[i])
        seg_w = CMP_STRIDE * GROUPS * HEAD_DIM
        kcmp, vcmp = _compress(kc.reshape(B, NS, seg_w), vc.reshape(B, NS, seg_w), wk, wv)

        o_nsa = _nsa(nsa_slopes, qn.reshape(B, S, 512), gates.reshape(B, S, 128), kcmp, vcmp,
                     nkv.reshape(B, S, 512), kaug, qfeat, ovt, C=256, TK=512)
        o_swa = _swa(sinks[i] * LOG2E, swa_qkv.reshape(B, S, 768), swa_bias, C=SWA_WINDOW)

        wo = w_o[i]
        wo_n = wo[:512][perm].astype(BF16)
        wo_s = wo[512:][perm].astype(BF16)
        x1 = _outproj(o_nsa.reshape(T, 512), o_swa.reshape(T, 512), xf, wo_n, wo_s, attn_post_g[i][None, :], tm)

        wgu = w_gate_up[i].astype(BF16)
        xf = _ffn(x1, p[i].reshape(T, PLE_DIM), mlp_pre_g[i][None, :], wgu[:, :D_FF], wgu[:, D_FF:],
                  conv_w[i], conv_b[i][None, :], w_down[i].astype(BF16), mlp_post_g[i][None, :],
                  w_ple[i].astype(BF16), w_ple_gate[i].astype(BF16), tm, S)
    return xf.reshape(B, S, D)
```

```python
import functools

import numpy as np
import jax
import jax.numpy as jnp
from jax import lax
from jax.experimental import pallas as pl
from jax.experimental.pallas import tpu as pltpu

F32 = jnp.float32
BF16 = jnp.bfloat16

D_MODEL = 1024
HEAD_DIM = 64
NSA_HEADS = 8
SWA_HEADS = 8
N_HEADS = NSA_HEADS + SWA_HEADS
CMP_LEN = 32
CMP_STRIDE = 16
CMP_HIDDEN = 128
SEL_LEN = 64
N_SEL = 16
NSA_WINDOW = 512
SWA_WINDOW = 128
D_FF = 2816
CONV_W = 3
PLE_DIM = 256
RMS_EPS = 1e-6
NEG_INF = -1e30
FORCE_BONUS = 1e6

LANES = 128
MASK_BIG = 2.0 ** 100
M_INIT = -3.0e38
GROUPS = 2
REP = 4
HEAD_PERM = (0, 4, 1, 5, 2, 6, 3, 7)
N_ALIBI_TERMS = 3
LOG2E = 1.4426950408889634

VMEM_LIMIT = 56 * 1024 * 1024


def _rms(x, g):
    return x * lax.rsqrt(jnp.mean(x * x, axis=-1, keepdims=True) + RMS_EPS) * g


def _gelu_tanh(x):
    c = np.sqrt(2.0 / np.pi).astype(np.float32)
    return x * (0.5 * (1.0 + jnp.tanh(c * (x + 0.044715 * (x * x * x)))))


def _dot(a, b):
    return jnp.dot(a, b, preferred_element_type=F32)


def _dot_nt(a, b):
    return lax.dot_general(a, b, (((1,), (1,)), ((), ())), preferred_element_type=F32)


def _rep_lanes(a, n):
    return a if n == 1 else jnp.concatenate([a] * n, axis=1)


def _params(sem):
    return pltpu.CompilerParams(dimension_semantics=sem, vmem_limit_bytes=VMEM_LIMIT)


def _inproj_body(x_ref, g_ref, w_ref, qn_ref, kc_ref, vc_ref, nkv_ref, swa_ref, gates_ref):
    h = _rms(x_ref[...], g_ref[...]).astype(BF16)
    proj = _dot(h, w_ref[...])
    qn_ref[...] = proj[:, 0:512].astype(BF16)
    kc_ref[...] = proj[:, 512:640].astype(BF16)
    vc_ref[...] = proj[:, 640:768].astype(BF16)
    nkv_ref[...] = proj[:, 768:1280].astype(BF16)
    swa_ref[...] = proj[:, 1280:2048].astype(BF16)
    gates_ref[...] = jax.nn.sigmoid(proj[:, 2048:2176])


def _inproj(x2d, g, w_cat, tm):
    T = x2d.shape[0]
    wn = w_cat.shape[1]
    row = lambda n: pl.BlockSpec((tm, n), lambda i: (i, 0))
    full = lambda a: pl.BlockSpec(a.shape, lambda i: (0,) * a.ndim)
    return pl.pallas_call(
        _inproj_body,
        grid=(T // tm,),
        in_specs=[row(D_MODEL), full(g), full(w_cat)],
        out_specs=[row(512), row(128), row(128), row(512), row(768), row(128)],
        out_shape=[
            jax.ShapeDtypeStruct((T, 512), BF16),
            jax.ShapeDtypeStruct((T, 128), BF16),
            jax.ShapeDtypeStruct((T, 128), BF16),
            jax.ShapeDtypeStruct((T, 512), BF16),
            jax.ShapeDtypeStruct((T, 768), BF16),
            jax.ShapeDtypeStruct((T, 128), F32),
        ],
        compiler_params=_params(("parallel",)),
        name="inproj",
    )(x2d, g, w_cat)


def _compress_body(kc_ref, vc_ref,
                   pea_k, peb_k, w1a_k, w1b_k, w2_k,
                   pea_v, peb_v, w1a_v, w1b_v, w2_v,
                   kcmp_ref, vcmp_ref):
    def one(seg_ref, pea, peb, w1a, w1b, w2, out_ref):
        seg = seg_ref[...].astype(F32)
        ns = seg.shape[0]
        ha = _dot((seg + pea[...]).astype(BF16), w1a[...])
        hb = _dot((seg + peb[...]).astype(BF16), w1b[...])
        hid = ha + pltpu.roll(hb, ns - 1, 0)
        act = _gelu_tanh(hid).astype(BF16)
        out_ref[...] = _dot(act, w2[...]).astype(BF16)

    one(kc_ref, pea_k, peb_k, w1a_k, w1b_k, w2_k, kcmp_ref)
    one(vc_ref, pea_v, peb_v, w1a_v, w1b_v, w2_v, vcmp_ref)


def _compress_weights(pe, w1, w2):
    half = CMP_LEN // 2
    eye = jnp.eye(GROUPS, dtype=F32)
    w1r = w1.reshape(CMP_LEN, HEAD_DIM, CMP_HIDDEN)

    def expand(w):
        return jnp.einsum("ldj,pg->lpdgj", w, eye).reshape(half * GROUPS * HEAD_DIM, GROUPS * CMP_HIDDEN)

    def pe_row(p):
        return jnp.broadcast_to(p[:, None, :], (half, GROUPS, HEAD_DIM)).reshape(1, half * GROUPS * HEAD_DIM)

    w2e = jnp.einsum("jd,pg->pjgd", w2, eye).reshape(GROUPS * CMP_HIDDEN, GROUPS * HEAD_DIM)
    return (pe_row(pe[:half]), pe_row(pe[half:]),
            expand(w1r[:half]).astype(BF16), expand(w1r[half:]).astype(BF16), w2e.astype(BF16))


def _compress(kc_seg, vc_seg, wk, wv):
    B, NS, WD = kc_seg.shape
    seg = pl.BlockSpec((None, NS, WD), lambda b: (b, 0, 0))
    full = lambda a: pl.BlockSpec(a.shape, lambda b: (0,) * a.ndim)
    out = pl.BlockSpec((None, NS, 128), lambda b: (b, 0, 0))
    return pl.pallas_call(
        _compress_body,
        grid=(B,),
        in_specs=[seg, seg] + [full(a) for a in wk] + [full(a) for a in wv],
        out_specs=[out, out],
        out_shape=[jax.ShapeDtypeStruct((B, NS, 128), BF16)] * 2,
        compiler_params=_params(("parallel",)),
        name="compress",
    )(kc_seg, vc_seg, *wk, *wv)


def _group_rows(q_all, g, C):
    lane = lax.broadcasted_iota(jnp.int32, (C, LANES), 1)
    keep = (lane < HEAD_DIM) if g == 0 else (lane >= HEAD_DIM)
    zero = jnp.zeros((C, LANES), q_all.dtype)
    return jnp.concatenate(
        [jnp.where(keep, q_all[:, r * LANES:(r + 1) * LANES], zero) for r in range(REP)], axis=0)


def _nsa_body(slopes_ref, qn_ref, gates_ref, kcmp_ref, vcmp_ref, ksl_ref, vsl_ref, kwn_ref, vwn_ref,
              kaug_ref, qfeat_ref, ovt_ref, out_ref,
              qa_ref, m_ref, acc_ref, part_ref, imp_ref, rank_ref, *, C, TK, W, NSB, NC, n_top):
    t0 = pl.program_id(1) * C
    q_all = qn_ref[...]
    qg = [_group_rows(q_all, g, C) for g in range(GROUPS)]
    gates = gates_ref[...]

    def softmax_rows(s):
        e = jnp.exp2(s - jnp.max(s, axis=-1, keepdims=True))
        return e * (1.0 / jnp.sum(e, axis=-1, keepdims=True))

    NW = W + C
    start = pl.multiple_of(jnp.maximum(t0 - W, 0), C)
    kw = jnp.concatenate([kwn_ref[pl.ds(start, NW), :], kaug_ref[pl.ds(start, NW), :]], axis=1)
    vw = jnp.concatenate([vwn_ref[pl.ds(start, NW), :], jnp.ones((NW, LANES), BF16)], axis=1)
    dist = (t0 + lax.broadcasted_iota(jnp.int32, (C, NW), 0)) - (start + lax.broadcasted_iota(jnp.int32, (C, NW), 1))
    wb = jnp.where((dist >= 0) & (dist < W), 0.0, NEG_INF)
    for g in range(GROUPS):
        feats = jnp.concatenate(
            [jnp.broadcast_to(qfeat_ref[g * REP + r:g * REP + r + 1, :], (C, LANES)) for r in range(REP)], axis=0)
        qw = jnp.concatenate([qg[g], feats.astype(BF16)], axis=1)
        s = _dot_nt(qw, kw)
        probs = []
        for r in range(REP):
            sr = s[r * C:(r + 1) * C] + wb
            probs.append(jnp.exp2(sr - jnp.max(sr, axis=-1, keepdims=True)).astype(BF16))
        ow = _dot(jnp.concatenate(probs, axis=0), vw)
        for r in range(REP):
            hcol = 3 * (g * REP + r)
            rows = slice(r * C, (r + 1) * C)
            part_ref[g, rows, :] = gates[:, hcol + 2:hcol + 3] * (ow[rows, :LANES] * (1.0 / ow[rows, LANES:]))

    tq_c = t0 + lax.broadcasted_iota(jnp.int32, (C, NC), 0)
    cend = lax.broadcasted_iota(jnp.int32, (C, NC), 1) * CMP_STRIDE + (CMP_LEN - 1)
    valid_c = cend <= tq_c
    cend_rel = (cend - t0).astype(F32)
    for g in range(GROUPS):
        sc = _dot_nt(qg[g], kcmp_ref[...])
        psum = jnp.zeros((C, NC), F32)
        probs = []
        for r in range(REP):
            s = sc[r * C:(r + 1) * C] + slopes_ref[g * REP + r] * cend_rel
            p = jnp.where(valid_c, softmax_rows(jnp.where(valid_c, s, NEG_INF)), 0.0)
            psum = psum + p
            probs.append(p.astype(BF16))
        o_cmp = _dot(jnp.concatenate(probs, axis=0), vcmp_ref[...])
        for r in range(REP):
            hcol = 3 * (g * REP + r)
            part_ref[g, r * C:(r + 1) * C, :] += gates[:, hcol:hcol + 1] * o_cmp[r * C:(r + 1) * C]

        ovt = ovt_ref[...]
        p_hi = psum.astype(BF16)
        r1 = psum - p_hi.astype(F32)
        p_mid = r1.astype(BF16)
        p_lo = (r1 - p_mid.astype(F32)).astype(BF16)
        imp = _dot_nt(ovt, p_hi) + _dot_nt(ovt, p_mid) + _dot_nt(ovt, p_lo)

        jblk = lax.broadcasted_iota(jnp.int32, (NSB, C), 0)
        jt = lax.shift_right_logical(t0 + lax.broadcasted_iota(jnp.int32, (NSB, C), 1), 6)
        forced = (jblk == 0) | (jblk == jt) | (jblk == jt - 1)
        visible = jblk <= jt
        imp = jnp.where(visible, jnp.where(forced, FORCE_BONUS, imp), NEG_INF)
        imp_ref[...] = imp
        rank_ref[...] = jnp.zeros(rank_ref.shape, F32)
        jrow = lax.broadcasted_iota(jnp.int32, (8, LANES), 0)

        def rank_pass(nblk):
            nv = nblk // 8
            for c0 in range(0, C, LANES):
                cols = slice(c0, c0 + LANES)
                blks = [imp_ref[8 * v:8 * v + 8, cols] for v in range(nv)]
                cnt = [jnp.zeros((8, LANES), F32) for _ in range(nv)]
                for i in range(nblk):
                    row = jnp.broadcast_to(imp_ref[i:i + 1, cols], (8, LANES))
                    for v in range(nv):
                        if i < 8 * v:
                            before = row >= blks[v]
                        elif i >= 8 * v + 8:
                            before = row > blks[v]
                        else:
                            before = (row > blks[v]) | ((row == blks[v]) & (jrow > i - 8 * v))
                        cnt[v] = cnt[v] + jnp.where(before, 1.0, 0.0)
                rank_ref[0:nblk, cols] = jnp.concatenate(cnt, axis=0)

        n_seen = (t0 + C) // SEL_LEN
        half = NSB // 2
        if half > n_top and half % 8 == 0:
            pl.when((n_seen > n_top) & (n_seen <= half))(functools.partial(rank_pass, half))
            pl.when(n_seen > half)(functools.partial(rank_pass, NSB))
        else:
            pl.when(n_seen > n_top)(functools.partial(rank_pass, NSB))

        selb_chunks = []
        for c0 in range(0, C, LANES):
            rank = rank_ref[:, c0:c0 + LANES]
            visible_c = (lax.broadcasted_iota(jnp.int32, (NSB, LANES), 0)
                         <= lax.shift_right_logical(t0 + c0 + lax.broadcasted_iota(jnp.int32, (NSB, LANES), 1), 6))
            sel_t = jnp.where(visible_c & (rank < n_top), 0.0, -MASK_BIG)
            sel_t = jnp.concatenate([sel_t, jnp.zeros((LANES - NSB, LANES), F32)], axis=0)
            selb_chunks.append(sel_t.T)
        selb = jnp.concatenate(selb_chunks, axis=0)
        for r in range(REP):
            feat = qfeat_ref[g * REP + r:g * REP + r + 1, :]
            qa_ref[g, r * C:(r + 1) * C, 0:LANES] = qg[g][r * C:(r + 1) * C]
            qa_ref[g, r * C:(r + 1) * C, LANES:2 * LANES] = (selb + feat).astype(BF16)

    assert TK == 2 * C
    tq = t0 + lax.broadcasted_iota(jnp.int32, (C, C), 0)
    kcol = lax.broadcasted_iota(jnp.int32, (C, C), 1)

    def load_tile(k0, width):
        kk = jnp.concatenate([ksl_ref[pl.ds(k0, width), :], kaug_ref[pl.ds(k0, width), :]], axis=1)
        vv = jnp.concatenate([vsl_ref[pl.ds(k0, width), :], jnp.ones((width, LANES), BF16)], axis=1)
        return kk, vv

    kk, vv = load_tile(pl.multiple_of(t0, C), C)
    cb = jnp.where(kcol <= lax.broadcasted_iota(jnp.int32, (C, C), 0), 0.0, NEG_INF)
    for g in range(GROUPS):
        s = _dot_nt(qa_ref[g], kk)
        probs = []
        for r in range(REP):
            rows = slice(r * C, (r + 1) * C)
            sr = s[rows] + cb
            m = jnp.max(sr, axis=-1, keepdims=True)
            m_ref[g, rows, :] = jnp.broadcast_to(m, (C, LANES))
            probs.append(jnp.exp2(sr - m).astype(BF16))
        acc_ref[0, g] = _dot(jnp.concatenate(probs, axis=0), vv)
    acc_ref[1] = jnp.zeros(acc_ref.shape[1:], F32)

    def past_tile(k0, width):
        kk, vv = load_tile(k0, width)
        outs = []
        for g in range(GROUPS):
            s = _dot_nt(qa_ref[g], kk)
            probs = [jnp.exp2(s[r * C:(r + 1) * C] - _rep_lanes(m_ref[g, r * C:(r + 1) * C, :], width // LANES))
                     .astype(BF16) for r in range(REP)]
            outs.append(_dot(jnp.concatenate(probs, axis=0), vv))
        return outs

    n_full = t0 // TK

    def pair_body(k, carry):
        a = past_tile(pl.multiple_of(2 * k * TK, TK), TK)
        b = past_tile(pl.multiple_of((2 * k + 1) * TK, TK), TK)
        for g in range(GROUPS):
            acc_ref[0, g] += a[g]
            acc_ref[1, g] += b[g]
        return carry

    lax.fori_loop(0, n_full // 2, pair_body, 0)

    @pl.when(n_full % 2 == 1)
    def _():
        a = past_tile(pl.multiple_of((n_full - 1) * TK, TK), TK)
        for g in range(GROUPS):
            acc_ref[0, g] += a[g]

    @pl.when(t0 % TK == C)
    def _():
        a = past_tile(pl.multiple_of(t0 - C, C), C)
        for g in range(GROUPS):
            acc_ref[1, g] += a[g]

    def tree_sum(parts):
        while len(parts) > 1:
            parts = [parts[i] + parts[i + 1] for i in range(0, len(parts), 2)]
        return parts[0]

    sums = tree_sum([acc_ref[a, g, 64 * j:64 * (j + 1), :]
                     for a in range(2) for g in range(GROUPS) for j in range(REP * C // 64)])

    @pl.when(jnp.logical_not(jnp.isfinite(jnp.sum(sums))))
    def _():
        m_ref[...] = jnp.full(m_ref.shape, M_INIT, F32)
        acc_ref[...] = jnp.zeros(acc_ref.shape, F32)

        def online_tile(kt, carry):
            k0 = pl.multiple_of(kt * C, C)
            kk, vv = load_tile(k0, C)
            cb = jnp.where(k0 + kcol <= tq, 0.0, NEG_INF)
            for g in range(GROUPS):
                s = _dot_nt(qa_ref[g], kk)
                probs = []
                for r in range(REP):
                    rows = slice(r * C, (r + 1) * C)
                    sr = s[rows] + cb
                    m_old = m_ref[g, rows, :]
                    m_new = jnp.maximum(m_old, jnp.max(sr, axis=-1, keepdims=True))
                    acc_ref[0, g, rows, :] = _rep_lanes(jnp.exp2(m_old - m_new), 2) * acc_ref[0, g, rows, :]
                    m_ref[g, rows, :] = m_new
                    probs.append(jnp.exp2(sr - _rep_lanes(m_new, C // LANES)).astype(BF16))
                acc_ref[0, g] += _dot(jnp.concatenate(probs, axis=0), vv)
            return carry

        lax.fori_loop(0, t0 // C + 1, online_tile, 0)

    lane = lax.broadcasted_iota(jnp.int32, (C, LANES), 1)
    for r in range(REP):
        chunk = None
        for g in range(GROUPS):
            hcol = 3 * (g * REP + r)
            rows = slice(r * C, (r + 1) * C)
            tot = acc_ref[0, g, rows, :] + acc_ref[1, g, rows, :]
            o_sel = tot[:, :LANES] * (1.0 / tot[:, LANES:])
            o = part_ref[g, rows, :] + gates[:, hcol + 1:hcol + 2] * o_sel
            chunk = o if g == 0 else jnp.where(lane < HEAD_DIM, chunk, o)
        out_ref[:, r * LANES:(r + 1) * LANES] = chunk.astype(BF16)


def _nsa(slopes, qn, gates, kcmp, vcmp, nkv, kaug, qfeat, ovt, *, C, TK):
    B, S, _ = qn.shape
    NC = kcmp.shape[1]
    NSB = S // SEL_LEN
    n_top = min(N_SEL, NSB)
    W = NSA_WINDOW
    body = functools.partial(_nsa_body, C=C, TK=TK, W=W, NSB=NSB, NC=NC, n_top=n_top)
    qblk = lambda n: pl.BlockSpec((None, C, n), lambda b, i: (b, i, 0))
    per_b = lambda n: pl.BlockSpec((None, n, 128), lambda b, i: (b, 0, 0))
    nkv_col = lambda j: pl.BlockSpec((None, S, 128), lambda b, i, j=j: (b, 0, j))
    full = lambda a: pl.BlockSpec(a.shape, lambda b, i: (0,) * a.ndim)
    return pl.pallas_call(
        body,
        grid=(B, S // C),
        in_specs=[pl.BlockSpec(memory_space=pltpu.SMEM),
                  qblk(512), qblk(128), per_b(NC), per_b(NC),
                  nkv_col(0), nkv_col(1), nkv_col(2), nkv_col(3),
                  full(kaug), full(qfeat), full(ovt)],
        out_specs=qblk(512),
        out_shape=jax.ShapeDtypeStruct((B, S, 512), BF16),
        scratch_shapes=[
            pltpu.VMEM((GROUPS, REP * C, 2 * LANES), BF16),
            pltpu.VMEM((GROUPS, REP * C, LANES), F32),
            pltpu.VMEM((2, GROUPS, REP * C, 2 * LANES), F32),
            pltpu.VMEM((GROUPS, REP * C, LANES), F32),
            pltpu.VMEM((NSB, C), F32),
            pltpu.VMEM((NSB, C), F32),
        ],
        compiler_params=_params(("parallel", "parallel")),
        name="nsa",
    )(slopes, qn, gates, kcmp, vcmp, nkv, nkv, nkv, nkv, kaug, qfeat, ovt)


def _swa_body(sinks_ref, q_ref, kp_ref, kc_ref, vp_ref, vc_ref, bias_ref, out_ref, *, C):
    WB = SWA_WINDOW
    at_start = jnp.where(pl.program_id(1) == 0, 1, 0)
    kall = jnp.concatenate([kp_ref[...], kc_ref[...]], axis=0)
    vall = jnp.concatenate([vp_ref[...], vc_ref[...]], axis=0)
    vall = jnp.concatenate([vall, jnp.ones((WB + C, LANES), BF16)], axis=1)
    lane = lax.broadcasted_iota(jnp.int32, (WB, LANES), 1)
    for sb in range(C // WB):
        q_sb = q_ref[sb * WB:(sb + 1) * WB, :]
        kk = kall[sb * WB:(sb + 2) * WB]
        vv = vall[sb * WB:(sb + 2) * WB]
        table = at_start if sb == 0 else 0
        outs = []
        for g in range(GROUPS):
            s_all = _dot_nt(_group_rows(q_sb, g, WB), kk)
            probs, sink_terms = [], []
            for r in range(REP):
                h = g * REP + r
                s = s_all[r * WB:(r + 1) * WB] + bias_ref[table, h]
                sink = sinks_ref[h]
                m = jnp.maximum(jnp.max(s, axis=-1, keepdims=True), sink)
                probs.append(jnp.exp2(s - m).astype(BF16))
                sink_terms.append(jnp.exp2(sink - m))
            o = _dot(jnp.concatenate(probs, axis=0), vv)
            outs.append([o[r * WB:(r + 1) * WB, :LANES] * (1.0 / (o[r * WB:(r + 1) * WB, LANES:] + sink_terms[r]))
                         for r in range(REP)])
        for r in range(REP):
            chunk = jnp.where(lane < HEAD_DIM, outs[0][r], outs[1][r])
            out_ref[sb * WB:(sb + 1) * WB, r * LANES:(r + 1) * LANES] = chunk.astype(BF16)


def _swa_bias(slopes_l2, C):
    qi = np.arange(C)[:, None]
    kj = np.arange(2 * C)[None, :]
    dist = C + qi - kj
    valid = (dist >= 0) & (dist < SWA_WINDOW)
    tab = -slopes_l2[:, None, None] * jnp.asarray(dist, F32)[None]
    return jnp.stack([jnp.where(valid[None], tab, NEG_INF), jnp.where((valid & (kj >= C))[None], tab, NEG_INF)])


def _swa(sinks_l2, swa_qkv, bias, *, C):
    B, S, _ = swa_qkv.shape
    body = functools.partial(_swa_body, C=C)
    smem = pl.BlockSpec(memory_space=pltpu.SMEM)
    q_spec = pl.BlockSpec((None, C, 512), lambda b, i: (b, i, 0))
    sub = C // SWA_WINDOW
    cur = lambda j: pl.BlockSpec((None, C, 128), lambda b, i, j=j: (b, i, j))
    prev = lambda j: pl.BlockSpec((None, SWA_WINDOW, 128), lambda b, i, j=j: (b, jnp.maximum(i * sub - 1, 0), j))
    return pl.pallas_call(
        body,
        grid=(B, S // C),
        in_specs=[smem, q_spec, prev(4), cur(4), prev(5), cur(5),
                  pl.BlockSpec(bias.shape, lambda b, i: (0, 0, 0, 0))],
        out_specs=pl.BlockSpec((None, C, 512), lambda b, i: (b, i, 0)),
        out_shape=jax.ShapeDtypeStruct((B, S, 512), BF16),
        compiler_params=_params(("parallel", "parallel")),
        name="swa",
    )(sinks_l2, swa_qkv, swa_qkv, swa_qkv, swa_qkv, swa_qkv, bias)


def _outproj_body(on_ref, os_ref, x_ref, wn_ref, ws_ref, g_ref, x1_ref):
    mix = _dot(on_ref[...], wn_ref[...]) + _dot(os_ref[...], ws_ref[...])
    x1_ref[...] = x_ref[...] + _rms(mix, g_ref[...])


def _outproj(o_nsa, o_swa, x2d, wn, ws, g, tm):
    T = x2d.shape[0]
    row = lambda n: pl.BlockSpec((tm, n), lambda i: (i, 0))
    full = lambda a: pl.BlockSpec(a.shape, lambda i: (0,) * a.ndim)
    return pl.pallas_call(
        _outproj_body,
        grid=(T // tm,),
        in_specs=[row(512), row(512), row(D_MODEL), full(wn), full(ws), full(g)],
        out_specs=row(D_MODEL),
        out_shape=jax.ShapeDtypeStruct((T, D_MODEL), F32),
        compiler_params=_params(("parallel",)),
        name="outproj",
    )(o_nsa, o_swa, x2d, wn, ws, g)


HALO = 8
FFN_CHUNK = 256


def _ffn_body(xh_ref, x_ref, p_ref, g1_ref, wg_ref, wu_ref, cw_ref, cb_ref, wd_ref, g2_ref,
              wple_ref, wpg_ref, out_ref, act_ref, *, tm, tiles_per_seq):
    first = (pl.program_id(0) % tiles_per_seq) == 0
    xh = jnp.where(first, 0.0, xh_ref[...])
    x = x_ref[...]
    g1 = g1_ref[...]
    he = jnp.concatenate([_rms(xh, g1), _rms(x, g1)], axis=0).astype(BF16)
    h = he[HALO:]
    for c in range(D_FF // FFN_CHUNK):
        cols = slice(c * FFN_CHUNK, (c + 1) * FFN_CHUNK)
        a = _dot(he, wg_ref[:, cols])
        u = _dot(h, wu_ref[:, cols])
        conv = (cw_ref[2:3, cols] * a[HALO:]
                + cw_ref[1:2, cols] * pltpu.roll(a, 1, 0)[HALO:]
                + cw_ref[0:1, cols] * pltpu.roll(a, 2, 0)[HALO:]
                + cb_ref[:, cols])
        act_ref[:, cols] = (_gelu_tanh(conv) * u).astype(BF16)
    y = _dot(act_ref[...], wd_ref[...])
    x2 = x + _rms(y, g2_ref[...])
    e = _dot(p_ref[...].astype(BF16), wple_ref[...])
    gate = jax.nn.sigmoid(_dot(x2.astype(BF16), wpg_ref[...]))
    out_ref[...] = x2 + e * gate


def _ffn(x1, p2d, g1, wg, wu, cw, cb, wd, g2, wple, wpg, tm, S):
    T = x1.shape[0]
    body = functools.partial(_ffn_body, tm=tm, tiles_per_seq=S // tm)
    row = lambda n: pl.BlockSpec((tm, n), lambda i: (i, 0))
    halo = pl.BlockSpec((HALO, D_MODEL), lambda i: (jnp.maximum(i * (tm // HALO) - 1, 0), 0))
    const = lambda a: pl.BlockSpec(a.shape, lambda i: (0,) * a.ndim, pipeline_mode=pl.Buffered(1))
    return pl.pallas_call(
        body,
        grid=(T // tm,),
        in_specs=[halo, row(D_MODEL), row(PLE_DIM), const(g1), const(wg), const(wu), const(cw), const(cb),
                  const(wd), const(g2), const(wple), const(wpg)],
        out_specs=row(D_MODEL),
        out_shape=jax.ShapeDtypeStruct((T, D_MODEL), F32),
        scratch_shapes=[pltpu.VMEM((tm, D_FF), BF16)],
        compiler_params=_params(("parallel",)),
        name="ffn",
    )(x1, x1, p2d, g1, wg, wu, cw, cb, wd, g2, wple, wpg)


def _alibi_slopes():
    h = jnp.arange(N_HEADS, dtype=F32)
    return jnp.exp2(-8.0 * (h + 1.0) / N_HEADS)


def _key_features(S):
    pos = np.arange(S)
    blk, off = pos // SEL_LEN, pos % SEL_LEN
    feat = np.zeros((S, LANES), np.float32)
    feat[pos, blk] = 1.0
    base = S // SEL_LEN
    assert base <= HEAD_DIM
    for k in range(N_ALIBI_TERMS):
        feat[:, HEAD_DIM + k] = blk * SEL_LEN
        feat[:, HEAD_DIM + N_ALIBI_TERMS + k] = off
    return jnp.asarray(feat, BF16)


def _query_features(slopes):
    terms, rest = [], slopes
    for _ in range(N_ALIBI_TERMS):
        t = rest.astype(BF16).astype(F32)
        terms.append(t)
        rest = rest - t
    tt = jnp.stack(terms, axis=1)
    feat = jnp.zeros((slopes.shape[0], LANES), F32)
    feat = feat.at[:, HEAD_DIM:HEAD_DIM + N_ALIBI_TERMS].set(tt)
    feat = feat.at[:, HEAD_DIM + N_ALIBI_TERMS:HEAD_DIM + 2 * N_ALIBI_TERMS].set(tt)
    return feat


def _overlap_t(S, NC):
    n = np.arange(NC)[None, :]
    j = np.arange(S // SEL_LEN)[:, None]
    cs, ss = n * CMP_STRIDE, j * SEL_LEN
    ov = (cs < ss + SEL_LEN) & (cs + CMP_LEN - 1 >= ss)
    return jnp.asarray(ov.astype(np.float32), BF16)


def _head_cols(perm):
    return np.concatenate([np.arange(h * HEAD_DIM, (h + 1) * HEAD_DIM) for h in perm])


def kernel(x, p, attn_pre_g, w_in, cmp_pe_k, cmp_w1_k, cmp_w2_k, cmp_pe_v, cmp_w1_v, cmp_w2_v, sinks, w_o,
           attn_post_g, mlp_pre_g, w_gate_up, conv_w, conv_b, w_down, mlp_post_g, w_ple, w_ple_gate):
    B, S, D = x.shape
    T = B * S
    depth = w_in.shape[0]
    slopes = _alibi_slopes()
    swa_slopes, nsa_slopes = slopes[:SWA_HEADS] * LOG2E, slopes[SWA_HEADS:] * LOG2E
    NS = S // CMP_STRIDE
    kaug = _key_features(S)
    qfeat = _query_features(nsa_slopes)
    ovt = _overlap_t(S, NS)
    swa_bias = _swa_bias(swa_slopes, SWA_WINDOW)
    perm = _head_cols(HEAD_PERM)
    scale = HEAD_DIM ** -0.5 * LOG2E
    tm = 512

    xf = x.reshape(T, D)
    for i in range(depth):
        w = w_in[i]
        q_n = w[:, 0:512][:, perm] * scale
        q_s = w[:, 1304:1816][:, perm] * scale
        gate_cols = jnp.pad(w[:, 1280:1304], ((0, 0), (0, LANES - 24)))
        w_cat = jnp.concatenate([q_n, w[:, 512:1280], q_s, w[:, 1816:2072], gate_cols], axis=1).astype(BF16)
        qn, kc, vc, nkv, swa_qkv, gates = _inproj(xf, attn_pre_g[i][None, :], w_cat, tm)

        wk = _compress_weights(cmp_pe_k[i], cmp_w1_k[i], cmp_w2_k[i])
        wv = _compress_weights(cmp_pe_v[i], cmp_w1_v[i], cmp_w2_v[i])
        seg_w = CMP_STRIDE * GROUPS * HEAD_DIM
        kcmp, vcmp = _compress(kc.reshape(B, NS, seg_w), vc.reshape(B, NS, seg_w), wk, wv)

        o_nsa = _nsa(nsa_slopes, qn.reshape(B, S, 512), gates.reshape(B, S, 128), kcmp, vcmp,
                     nkv.reshape(B, S, 512), kaug, qfeat, ovt, C=256, TK=512)
        o_swa = _swa(sinks[i] * LOG2E, swa_qkv.reshape(B, S, 768), swa_bias, C=2 * SWA_WINDOW)

        wo = w_o[i]
        wo_n = wo[:512][perm].astype(BF16)
        wo_s = wo[512:][perm].astype(BF16)
        x1 = _outproj(o_nsa.reshape(T, 512), o_swa.reshape(T, 512), xf, wo_n, wo_s, attn_post_g[i][None, :], tm)

        wgu = w_gate_up[i].astype(BF16)
        xf = _ffn(x1, p[i].reshape(T, PLE_DIM), mlp_pre_g[i][None, :], wgu[:, :D_FF], wgu[:, D_FF:],
                  conv_w[i], conv_b[i][None, :], w_down[i].astype(BF16), mlp_post_g[i][None, :],
                  w_ple[i].astype(BF16), w_ple_gate[i].astype(BF16), tm, S)
    return xf.reshape(B, S, D)
```

```python
import functools

import numpy as np
import jax
import jax.numpy as jnp
from jax import lax
from jax.experimental import pallas as pl
from jax.experimental.pallas import tpu as pltpu

F32 = jnp.float32
BF16 = jnp.bfloat16

D_MODEL = 1024
HEAD_DIM = 64
NSA_HEADS = 8
SWA_HEADS = 8
N_HEADS = NSA_HEADS + SWA_HEADS
CMP_LEN = 32
CMP_STRIDE = 16
CMP_HIDDEN = 128
SEL_LEN = 64
N_SEL = 16
NSA_WINDOW = 512
SWA_WINDOW = 128
D_FF = 2816
CONV_W = 3
PLE_DIM = 256
RMS_EPS = 1e-6
NEG_INF = -1e30
FORCE_BONUS = 1e6

LANES = 128
MASK_BIG = 2.0 ** 100
M_INIT = -3.0e38
GROUPS = 2
REP = 4
HEAD_PERM = (0, 4, 1, 5, 2, 6, 3, 7)
N_ALIBI_TERMS = 3
LOG2E = 1.4426950408889634

VMEM_LIMIT = 56 * 1024 * 1024


def _rms(x, g):
    return x * lax.rsqrt(jnp.mean(x * x, axis=-1, keepdims=True) + RMS_EPS) * g


def _gelu_tanh(x):
    c = np.sqrt(2.0 / np.pi).astype(np.float32)
    return x * (0.5 * (1.0 + jnp.tanh(c * (x + 0.044715 * (x * x * x)))))


def _dot(a, b):
    return jnp.dot(a, b, preferred_element_type=F32)


def _dot_nt(a, b):
    return lax.dot_general(a, b, (((1,), (1,)), ((), ())), preferred_element_type=F32)


def _rep_lanes(a, n):
    return a if n == 1 else jnp.concatenate([a] * n, axis=1)


def _params(sem):
    return pltpu.CompilerParams(dimension_semantics=sem, vmem_limit_bytes=VMEM_LIMIT)


def _inproj_body(x_ref, g_ref, w_ref, qn_ref, kc_ref, vc_ref, nkv_ref, swa_ref, gates_ref):
    h = _rms(x_ref[...], g_ref[...]).astype(BF16)
    proj = _dot(h, w_ref[...])
    qn_ref[...] = proj[:, 0:512].astype(BF16)
    kc_ref[...] = proj[:, 512:640].astype(BF16)
    vc_ref[...] = proj[:, 640:768].astype(BF16)
    nkv_ref[...] = proj[:, 768:1280].astype(BF16)
    swa_ref[...] = proj[:, 1280:2048].astype(BF16)
    gates_ref[...] = jax.nn.sigmoid(proj[:, 2048:2176])


def _inproj(x2d, g, w_cat, tm):
    T = x2d.shape[0]
    wn = w_cat.shape[1]
    row = lambda n: pl.BlockSpec((tm, n), lambda i: (i, 0))
    full = lambda a: pl.BlockSpec(a.shape, lambda i: (0,) * a.ndim)
    return pl.pallas_call(
        _inproj_body,
        grid=(T // tm,),
        in_specs=[row(D_MODEL), full(g), full(w_cat)],
        out_specs=[row(512), row(128), row(128), row(512), row(768), row(128)],
        out_shape=[
            jax.ShapeDtypeStruct((T, 512), BF16),
            jax.ShapeDtypeStruct((T, 128), BF16),
            jax.ShapeDtypeStruct((T, 128), BF16),
            jax.ShapeDtypeStruct((T, 512), BF16),
            jax.ShapeDtypeStruct((T, 768), BF16),
            jax.ShapeDtypeStruct((T, 128), F32),
        ],
        compiler_params=_params(("parallel",)),
        name="inproj",
    )(x2d, g, w_cat)


def _compress_body(kc_ref, vc_ref,
                   pea_k, peb_k, w1a_k, w1b_k, w2_k,
                   pea_v, peb_v, w1a_v, w1b_v, w2_v,
                   kcmp_ref, vcmp_ref):
    def one(seg_ref, pea, peb, w1a, w1b, w2, out_ref):
        seg = seg_ref[...].astype(F32)
        ns = seg.shape[0]
        ha = _dot((seg + pea[...]).astype(BF16), w1a[...])
        hb = _dot((seg + peb[...]).astype(BF16), w1b[...])
        hid = ha + pltpu.roll(hb, ns - 1, 0)
        act = _gelu_tanh(hid).astype(BF16)
        out_ref[...] = _dot(act, w2[...]).astype(BF16)

    one(kc_ref, pea_k, peb_k, w1a_k, w1b_k, w2_k, kcmp_ref)
    one(vc_ref, pea_v, peb_v, w1a_v, w1b_v, w2_v, vcmp_ref)


def _compress_weights(pe, w1, w2):
    half = CMP_LEN // 2
    eye = jnp.eye(GROUPS, dtype=F32)
    w1r = w1.reshape(CMP_LEN, HEAD_DIM, CMP_HIDDEN)

    def expand(w):
        return jnp.einsum("ldj,pg->lpdgj", w, eye).reshape(half * GROUPS * HEAD_DIM, GROUPS * CMP_HIDDEN)

    def pe_row(p):
        return jnp.broadcast_to(p[:, None, :], (half, GROUPS, HEAD_DIM)).reshape(1, half * GROUPS * HEAD_DIM)

    w2e = jnp.einsum("jd,pg->pjgd", w2, eye).reshape(GROUPS * CMP_HIDDEN, GROUPS * HEAD_DIM)
    return (pe_row(pe[:half]), pe_row(pe[half:]),
            expand(w1r[:half]).astype(BF16), expand(w1r[half:]).astype(BF16), w2e.astype(BF16))


def _compress(kc_seg, vc_seg, wk, wv):
    B, NS, WD = kc_seg.shape
    seg = pl.BlockSpec((None, NS, WD), lambda b: (b, 0, 0))
    full = lambda a: pl.BlockSpec(a.shape, lambda b: (0,) * a.ndim)
    out = pl.BlockSpec((None, NS, 128), lambda b: (b, 0, 0))
    return pl.pallas_call(
        _compress_body,
        grid=(B,),
        in_specs=[seg, seg] + [full(a) for a in wk] + [full(a) for a in wv],
        out_specs=[out, out],
        out_shape=[jax.ShapeDtypeStruct((B, NS, 128), BF16)] * 2,
        compiler_params=_params(("parallel",)),
        name="compress",
    )(kc_seg, vc_seg, *wk, *wv)


def _group_rows(q_all, g, C):
    lane = lax.broadcasted_iota(jnp.int32, (C, LANES), 1)
    keep = (lane < HEAD_DIM) if g == 0 else (lane >= HEAD_DIM)
    zero = jnp.zeros((C, LANES), q_all.dtype)
    return jnp.concatenate(
        [jnp.where(keep, q_all[:, r * LANES:(r + 1) * LANES], zero) for r in range(REP)], axis=0)


def _nsa_body(slopes_ref, qn_ref, gates_ref, kcmp_ref, vcmp_ref, ksl_ref, vsl_ref, kwn_ref, vwn_ref,
              kaug_ref, qfeat_ref, ovt_ref, out_ref,
              qa_ref, m_ref, acc_ref, win_ref, part_ref, imp_ref, rank_ref, *, C, TK, W, NSB, NC, n_top):
    t0 = pl.program_id(1) * C
    q_all = qn_ref[...]
    qg = [_group_rows(q_all, g, C) for g in range(GROUPS)]
    gates = gates_ref[...]

    def softmax_rows(s):
        e = jnp.exp2(s - jnp.max(s, axis=-1, keepdims=True))
        return e * (1.0 / jnp.sum(e, axis=-1, keepdims=True))

    assert C <= W
    rowi = lax.broadcasted_iota(jnp.int32, (C, C), 0)
    coli = lax.broadcasted_iota(jnp.int32, (C, C), 1)
    causal_bias = jnp.where(coli <= rowi, 0.0, NEG_INF)
    pstart = pl.multiple_of(jnp.maximum(t0 - W, 0), C)
    ppos = pstart + lax.broadcasted_iota(jnp.int32, (C, W), 1)
    pdist = (t0 + lax.broadcasted_iota(jnp.int32, (C, W), 0)) - ppos
    past_bias = jnp.where((pdist < W) & (ppos < t0), 0.0, NEG_INF)

    def win_operands():
        td = pl.multiple_of(t0, C)
        kd = jnp.concatenate([kwn_ref[pl.ds(td, C), :], kaug_ref[pl.ds(td, C), :]], axis=1)
        kp = jnp.concatenate([kwn_ref[pl.ds(pstart, W), :], kaug_ref[pl.ds(pstart, W), :]], axis=1)
        vals = jnp.concatenate([vwn_ref[pl.ds(pstart, W), :], vwn_ref[pl.ds(td, C), :]], axis=0)
        return kd, kp, jnp.concatenate([vals, jnp.ones((W + C, LANES), BF16)], axis=1)

    def win_queries(g):
        feats = jnp.concatenate(
            [jnp.broadcast_to(qfeat_ref[g * REP + r:g * REP + r + 1, :], (C, LANES)) for r in range(REP)], axis=0)
        return jnp.concatenate([qg[g], feats.astype(BF16)], axis=1)

    kd, kp, vw = win_operands()
    for g in range(GROUPS):
        qw = win_queries(g)
        sd = _dot_nt(qw, kd)
        sp = _dot_nt(qw, kp)
        probs = []
        for r in range(REP):
            rows = slice(r * C, (r + 1) * C)
            srd = sd[rows] + causal_bias
            m = jnp.max(srd, axis=-1, keepdims=True)
            probs.append(jnp.concatenate([jnp.exp2(sp[rows] + past_bias - m), jnp.exp2(srd - m)], axis=1).astype(BF16))
        win_ref[g] = _dot(jnp.concatenate(probs, axis=0), vw)

    tq_c = t0 + lax.broadcasted_iota(jnp.int32, (C, NC), 0)
    cend = lax.broadcasted_iota(jnp.int32, (C, NC), 1) * CMP_STRIDE + (CMP_LEN - 1)
    valid_c = cend <= tq_c
    cend_rel = (cend - t0).astype(F32)
    for g in range(GROUPS):
        sc = _dot_nt(qg[g], kcmp_ref[...])
        psum = jnp.zeros((C, NC), F32)
        probs = []
        for r in range(REP):
            s = sc[r * C:(r + 1) * C] + slopes_ref[g * REP + r] * cend_rel
            p = jnp.where(valid_c, softmax_rows(jnp.where(valid_c, s, NEG_INF)), 0.0)
            psum = psum + p
            probs.append(p.astype(BF16))
        o_cmp = _dot(jnp.concatenate(probs, axis=0), vcmp_ref[...])
        for r in range(REP):
            hcol = 3 * (g * REP + r)
            part_ref[g, r * C:(r + 1) * C, :] = gates[:, hcol:hcol + 1] * o_cmp[r * C:(r + 1) * C]

        ovt = ovt_ref[...]
        p_hi = psum.astype(BF16)
        r1 = psum - p_hi.astype(F32)
        p_mid = r1.astype(BF16)
        p_lo = (r1 - p_mid.astype(F32)).astype(BF16)
        imp = _dot_nt(ovt, p_hi) + _dot_nt(ovt, p_mid) + _dot_nt(ovt, p_lo)

        jblk = lax.broadcasted_iota(jnp.int32, (NSB, C), 0)
        jt = lax.shift_right_logical(t0 + lax.broadcasted_iota(jnp.int32, (NSB, C), 1), 6)
        forced = (jblk == 0) | (jblk == jt) | (jblk == jt - 1)
        visible = jblk <= jt
        imp = jnp.where(visible, jnp.where(forced, FORCE_BONUS, imp), NEG_INF)
        imp_ref[...] = imp
        rank_ref[...] = jnp.zeros(rank_ref.shape, F32)
        jrow = lax.broadcasted_iota(jnp.int32, (8, LANES), 0)

        def rank_pass(nblk):
            nv = nblk // 8
            for c0 in range(0, C, LANES):
                cols = slice(c0, c0 + LANES)
                blks = [imp_ref[8 * v:8 * v + 8, cols] for v in range(nv)]
                cnt = [jnp.zeros((8, LANES), F32) for _ in range(nv)]
                for i in range(nblk):
                    row = jnp.broadcast_to(imp_ref[i:i + 1, cols], (8, LANES))
                    for v in range(nv):
                        if i < 8 * v:
                            before = row >= blks[v]
                        elif i >= 8 * v + 8:
                            before = row > blks[v]
                        else:
                            before = (row > blks[v]) | ((row == blks[v]) & (jrow > i - 8 * v))
                        cnt[v] = cnt[v] + jnp.where(before, 1.0, 0.0)
                rank_ref[0:nblk, cols] = jnp.concatenate(cnt, axis=0)

        n_seen = (t0 + C) // SEL_LEN
        half = NSB // 2
        if half > n_top and half % 8 == 0:
            pl.when((n_seen > n_top) & (n_seen <= half))(functools.partial(rank_pass, half))
            pl.when(n_seen > half)(functools.partial(rank_pass, NSB))
        else:
            pl.when(n_seen > n_top)(functools.partial(rank_pass, NSB))

        selb_chunks = []
        for c0 in range(0, C, LANES):
            rank = rank_ref[:, c0:c0 + LANES]
            visible_c = (lax.broadcasted_iota(jnp.int32, (NSB, LANES), 0)
                         <= lax.shift_right_logical(t0 + c0 + lax.broadcasted_iota(jnp.int32, (NSB, LANES), 1), 6))
            sel_t = jnp.where(visible_c & (rank < n_top), 0.0, -MASK_BIG)
            sel_t = jnp.concatenate([sel_t, jnp.zeros((LANES - NSB, LANES), F32)], axis=0)
            selb_chunks.append(sel_t.T)
        selb = jnp.concatenate(selb_chunks, axis=0)
        for r in range(REP):
            feat = qfeat_ref[g * REP + r:g * REP + r + 1, :]
            qa_ref[g, r * C:(r + 1) * C, 0:LANES] = qg[g][r * C:(r + 1) * C]
            qa_ref[g, r * C:(r + 1) * C, LANES:2 * LANES] = (selb + feat).astype(BF16)

    assert TK == 2 * C
    tq = t0 + lax.broadcasted_iota(jnp.int32, (C, C), 0)
    kcol = lax.broadcasted_iota(jnp.int32, (C, C), 1)

    def load_tile(k0, width):
        kk = jnp.concatenate([ksl_ref[pl.ds(k0, width), :], kaug_ref[pl.ds(k0, width), :]], axis=1)
        vv = jnp.concatenate([vsl_ref[pl.ds(k0, width), :], jnp.ones((width, LANES), BF16)], axis=1)
        return kk, vv

    kk, vv = load_tile(pl.multiple_of(t0, C), C)
    cb = jnp.where(kcol <= lax.broadcasted_iota(jnp.int32, (C, C), 0), 0.0, NEG_INF)
    for g in range(GROUPS):
        s = _dot_nt(qa_ref[g], kk)
        probs = []
        for r in range(REP):
            rows = slice(r * C, (r + 1) * C)
            sr = s[rows] + cb
            m = jnp.max(sr, axis=-1, keepdims=True)
            m_ref[g, rows, :] = jnp.broadcast_to(m, (C, LANES))
            probs.append(jnp.exp2(sr - m).astype(BF16))
        acc_ref[0, g] = _dot(jnp.concatenate(probs, axis=0), vv)
    acc_ref[1] = jnp.zeros(acc_ref.shape[1:], F32)

    def past_tile(k0, width):
        kk, vv = load_tile(k0, width)
        outs = []
        for g in range(GROUPS):
            s = _dot_nt(qa_ref[g], kk)
            probs = [jnp.exp2(s[r * C:(r + 1) * C] - _rep_lanes(m_ref[g, r * C:(r + 1) * C, :], width // LANES))
                     .astype(BF16) for r in range(REP)]
            outs.append(_dot(jnp.concatenate(probs, axis=0), vv))
        return outs

    n_full = t0 // TK

    def pair_body(k, carry):
        a = past_tile(pl.multiple_of(2 * k * TK, TK), TK)
        b = past_tile(pl.multiple_of((2 * k + 1) * TK, TK), TK)
        for g in range(GROUPS):
            acc_ref[0, g] += a[g]
            acc_ref[1, g] += b[g]
        return carry

    lax.fori_loop(0, n_full // 2, pair_body, 0)

    @pl.when(n_full % 2 == 1)
    def _():
        a = past_tile(pl.multiple_of((n_full - 1) * TK, TK), TK)
        for g in range(GROUPS):
            acc_ref[0, g] += a[g]

    @pl.when(t0 % TK == C)
    def _():
        a = past_tile(pl.multiple_of(t0 - C, C), C)
        for g in range(GROUPS):
            acc_ref[1, g] += a[g]

    def tree_sum(parts):
        while len(parts) > 1:
            parts = [parts[i] + parts[i + 1] for i in range(0, len(parts), 2)]
        return parts[0]

    def combine():
        lane = lax.broadcasted_iota(jnp.int32, (C, LANES), 1)
        seen = []
        for r in range(REP):
            chunk = None
            for g in range(GROUPS):
                hcol = 3 * (g * REP + r)
                rows = slice(r * C, (r + 1) * C)
                sel = acc_ref[0, g, rows, :] + acc_ref[1, g, rows, :]
                win = win_ref[g, rows, :]
                seen += [sel, win]
                o = (part_ref[g, rows, :]
                     + gates[:, hcol + 1:hcol + 2] * (sel[:, :LANES] * (1.0 / sel[:, LANES:]))
                     + gates[:, hcol + 2:hcol + 3] * (win[:, :LANES] * (1.0 / win[:, LANES:])))
                chunk = o if g == 0 else jnp.where(lane < HEAD_DIM, chunk, o)
            out_ref[:, r * LANES:(r + 1) * LANES] = chunk.astype(BF16)
        return jnp.sum(tree_sum(seen))

    @pl.when(jnp.logical_not(jnp.isfinite(combine())))
    def _():
        kd, kp, vw = win_operands()
        for g in range(GROUPS):
            qw = win_queries(g)
            sd = _dot_nt(qw, kd)
            sp = _dot_nt(qw, kp)
            probs = []
            for r in range(REP):
                rows = slice(r * C, (r + 1) * C)
                sr = jnp.concatenate([sp[rows] + past_bias, sd[rows] + causal_bias], axis=1)
                probs.append(jnp.exp2(sr - jnp.max(sr, axis=-1, keepdims=True)).astype(BF16))
            win_ref[g] = _dot(jnp.concatenate(probs, axis=0), vw)

        m_ref[...] = jnp.full(m_ref.shape, M_INIT, F32)
        acc_ref[...] = jnp.zeros(acc_ref.shape, F32)

        def online_tile(kt, carry):
            k0 = pl.multiple_of(kt * C, C)
            kk, vv = load_tile(k0, C)
            cb = jnp.where(k0 + kcol <= tq, 0.0, NEG_INF)
            for g in range(GROUPS):
                s = _dot_nt(qa_ref[g], kk)
                probs = []
                for r in range(REP):
                    rows = slice(r * C, (r + 1) * C)
                    sr = s[rows] + cb
                    m_old = m_ref[g, rows, :]
                    m_new = jnp.maximum(m_old, jnp.max(sr, axis=-1, keepdims=True))
                    acc_ref[0, g, rows, :] = _rep_lanes(jnp.exp2(m_old - m_new), 2) * acc_ref[0, g, rows, :]
                    m_ref[g, rows, :] = m_new
                    probs.append(jnp.exp2(sr - _rep_lanes(m_new, C // LANES)).astype(BF16))
                acc_ref[0, g] += _dot(jnp.concatenate(probs, axis=0), vv)
            return carry

        lax.fori_loop(0, t0 // C + 1, online_tile, 0)
        combine()


def _nsa(slopes, qn, gates, kcmp, vcmp, nkv, kaug, qfeat, ovt, *, C, TK):
    B, S, _ = qn.shape
    NC = kcmp.shape[1]
    NSB = S // SEL_LEN
    n_top = min(N_SEL, NSB)
    W = NSA_WINDOW
    body = functools.partial(_nsa_body, C=C, TK=TK, W=W, NSB=NSB, NC=NC, n_top=n_top)
    qblk = lambda n: pl.BlockSpec((None, C, n), lambda b, i: (b, i, 0))
    per_b = lambda n: pl.BlockSpec((None, n, 128), lambda b, i: (b, 0, 0))
    nkv_col = lambda j: pl.BlockSpec((None, S, 128), lambda b, i, j=j: (b, 0, j))
    full = lambda a: pl.BlockSpec(a.shape, lambda b, i: (0,) * a.ndim)
    return pl.pallas_call(
        body,
        grid=(B, S // C),
        in_specs=[pl.BlockSpec(memory_space=pltpu.SMEM),
                  qblk(512), qblk(128), per_b(NC), per_b(NC),
                  nkv_col(0), nkv_col(1), nkv_col(2), nkv_col(3),
                  full(kaug), full(qfeat), full(ovt)],
        out_specs=qblk(512),
        out_shape=jax.ShapeDtypeStruct((B, S, 512), BF16),
        scratch_shapes=[
            pltpu.VMEM((GROUPS, REP * C, 2 * LANES), BF16),
            pltpu.VMEM((GROUPS, REP * C, LANES), F32),
            pltpu.VMEM((2, GROUPS, REP * C, 2 * LANES), F32),
            pltpu.VMEM((GROUPS, REP * C, 2 * LANES), F32),
            pltpu.VMEM((GROUPS, REP * C, LANES), F32),
            pltpu.VMEM((NSB, C), F32),
            pltpu.VMEM((NSB, C), F32),
        ],
        compiler_params=_params(("parallel", "parallel")),
        name="nsa",
    )(slopes, qn, gates, kcmp, vcmp, nkv, nkv, nkv, nkv, kaug, qfeat, ovt)


def _swa_body(sinks_ref, q_ref, kp_ref, kc_ref, vp_ref, vc_ref, bias_ref, out_ref, *, C):
    WB = SWA_WINDOW
    at_start = jnp.where(pl.program_id(1) == 0, 1, 0)
    kall = jnp.concatenate([kp_ref[...], kc_ref[...]], axis=0)
    vall = jnp.concatenate([vp_ref[...], vc_ref[...]], axis=0)
    vall = jnp.concatenate([vall, jnp.ones((WB + C, LANES), BF16)], axis=1)
    lane = lax.broadcasted_iota(jnp.int32, (WB, LANES), 1)
    for sb in range(C // WB):
        q_sb = q_ref[sb * WB:(sb + 1) * WB, :]
        kk = kall[sb * WB:(sb + 2) * WB]
        vv = vall[sb * WB:(sb + 2) * WB]
        table = at_start if sb == 0 else 0
        outs = []
        for g in range(GROUPS):
            s_all = _dot_nt(_group_rows(q_sb, g, WB), kk)
            probs, sink_terms = [], []
            for r in range(REP):
                h = g * REP + r
                s = s_all[r * WB:(r + 1) * WB] + bias_ref[table, h]
                sink = sinks_ref[h]
                m = jnp.maximum(jnp.max(s, axis=-1, keepdims=True), sink)
                probs.append(jnp.exp2(s - m).astype(BF16))
                sink_terms.append(jnp.exp2(sink - m))
            o = _dot(jnp.concatenate(probs, axis=0), vv)
            outs.append([o[r * WB:(r + 1) * WB, :LANES] * (1.0 / (o[r * WB:(r + 1) * WB, LANES:] + sink_terms[r]))
                         for r in range(REP)])
        for r in range(REP):
            chunk = jnp.where(lane < HEAD_DIM, outs[0][r], outs[1][r])
            out_ref[sb * WB:(sb + 1) * WB, r * LANES:(r + 1) * LANES] = chunk.astype(BF16)


def _swa_bias(slopes_l2, C):
    qi = np.arange(C)[:, None]
    kj = np.arange(2 * C)[None, :]
    dist = C + qi - kj
    valid = (dist >= 0) & (dist < SWA_WINDOW)
    tab = -slopes_l2[:, None, None] * jnp.asarray(dist, F32)[None]
    return jnp.stack([jnp.where(valid[None], tab, NEG_INF), jnp.where((valid & (kj >= C))[None], tab, NEG_INF)])


def _swa(sinks_l2, swa_qkv, bias, *, C):
    B, S, _ = swa_qkv.shape
    body = functools.partial(_swa_body, C=C)
    smem = pl.BlockSpec(memory_space=pltpu.SMEM)
    q_spec = pl.BlockSpec((None, C, 512), lambda b, i: (b, i, 0))
    sub = C // SWA_WINDOW
    cur = lambda j: pl.BlockSpec((None, C, 128), lambda b, i, j=j: (b, i, j))
    prev = lambda j: pl.BlockSpec((None, SWA_WINDOW, 128), lambda b, i, j=j: (b, jnp.maximum(i * sub - 1, 0), j))
    return pl.pallas_call(
        body,
        grid=(B, S // C),
        in_specs=[smem, q_spec, prev(4), cur(4), prev(5), cur(5),
                  pl.BlockSpec(bias.shape, lambda b, i: (0, 0, 0, 0))],
        out_specs=pl.BlockSpec((None, C, 512), lambda b, i: (b, i, 0)),
        out_shape=jax.ShapeDtypeStruct((B, S, 512), BF16),
        compiler_params=_params(("parallel", "parallel")),
        name="swa",
    )(sinks_l2, swa_qkv, swa_qkv, swa_qkv, swa_qkv, swa_qkv, bias)


CONV_TAIL = 8
FFN_CHUNK = 256


def _ffn_body(on_ref, os_ref, x_ref, p_ref, wn_ref, ws_ref, g0_ref, g1_ref, wg_ref, wu_ref, cw_ref, cb_ref,
              wd_ref, g2_ref, wple_ref, wpg_ref, out_ref, act_ref, tail_ref, *, tm, tiles_per_seq):
    @pl.when(pl.program_id(0) % tiles_per_seq == 0)
    def _():
        tail_ref[...] = jnp.zeros(tail_ref.shape, F32)

    mix = _dot(on_ref[...], wn_ref[...]) + _dot(os_ref[...], ws_ref[...])
    x1 = x_ref[...] + _rms(mix, g0_ref[...])
    h = _rms(x1, g1_ref[...]).astype(BF16)
    for c in range(D_FF // FFN_CHUNK):
        cols = slice(c * FFN_CHUNK, (c + 1) * FFN_CHUNK)
        a = _dot(h, wg_ref[:, cols])
        u = _dot(h, wu_ref[:, cols])
        ae = jnp.concatenate([tail_ref[:, cols], a], axis=0)
        tail_ref[:, cols] = a[tm - CONV_TAIL:]
        conv = (cw_ref[2:3, cols] * a
                + cw_ref[1:2, cols] * pltpu.roll(ae, 1, 0)[CONV_TAIL:]
                + cw_ref[0:1, cols] * pltpu.roll(ae, 2, 0)[CONV_TAIL:]
                + cb_ref[:, cols])
        act_ref[:, cols] = (_gelu_tanh(conv) * u).astype(BF16)
    y = _dot(act_ref[...], wd_ref[...])
    x2 = x1 + _rms(y, g2_ref[...])
    e = _dot(p_ref[...].astype(BF16), wple_ref[...])
    gate = jax.nn.sigmoid(_dot(x2.astype(BF16), wpg_ref[...]))
    out_ref[...] = x2 + e * gate


def _ffn(o_nsa, o_swa, x2d, p2d, wn, ws, g0, g1, wg, wu, cw, cb, wd, g2, wple, wpg, tm, S):
    T = x2d.shape[0]
    body = functools.partial(_ffn_body, tm=tm, tiles_per_seq=S // tm)
    row = lambda n: pl.BlockSpec((tm, n), lambda i: (i, 0))
    const = lambda a: pl.BlockSpec(a.shape, lambda i: (0,) * a.ndim, pipeline_mode=pl.Buffered(1))
    weights = (wn, ws, g0, g1, wg, wu, cw, cb, wd, g2, wple, wpg)
    return pl.pallas_call(
        body,
        grid=(T // tm,),
        in_specs=[row(512), row(512), row(D_MODEL), row(PLE_DIM)] + [const(a) for a in weights],
        out_specs=row(D_MODEL),
        out_shape=jax.ShapeDtypeStruct((T, D_MODEL), F32),
        scratch_shapes=[pltpu.VMEM((tm, D_FF), BF16), pltpu.VMEM((CONV_TAIL, D_FF), F32)],
        compiler_params=_params(("arbitrary",)),
        name="ffn",
    )(o_nsa, o_swa, x2d, p2d, *weights)


def _alibi_slopes():
    h = jnp.arange(N_HEADS, dtype=F32)
    return jnp.exp2(-8.0 * (h + 1.0) / N_HEADS)


def _key_features(S):
    pos = np.arange(S)
    blk, off = pos // SEL_LEN, pos % SEL_LEN
    feat = np.zeros((S, LANES), np.float32)
    feat[pos, blk] = 1.0
    base = S // SEL_LEN
    assert base <= HEAD_DIM
    for k in range(N_ALIBI_TERMS):
        feat[:, HEAD_DIM + k] = blk * SEL_LEN
        feat[:, HEAD_DIM + N_ALIBI_TERMS + k] = off
    return jnp.asarray(feat, BF16)


def _query_features(slopes):
    terms, rest = [], slopes
    for _ in range(N_ALIBI_TERMS):
        t = rest.astype(BF16).astype(F32)
        terms.append(t)
        rest = rest - t
    tt = jnp.stack(terms, axis=1)
    feat = jnp.zeros((slopes.shape[0], LANES), F32)
    feat = feat.at[:, HEAD_DIM:HEAD_DIM + N_ALIBI_TERMS].set(tt)
    feat = feat.at[:, HEAD_DIM + N_ALIBI_TERMS:HEAD_DIM + 2 * N_ALIBI_TERMS].set(tt)
    return feat


def _overlap_t(S, NC):
    n = np.arange(NC)[None, :]
    j = np.arange(S // SEL_LEN)[:, None]
    cs, ss = n * CMP_STRIDE, j * SEL_LEN
    ov = (cs < ss + SEL_LEN) & (cs + CMP_LEN - 1 >= ss)
    return jnp.asarray(ov.astype(np.float32), BF16)


def _head_cols(perm):
    return np.concatenate([np.arange(h * HEAD_DIM, (h + 1) * HEAD_DIM) for h in perm])


def kernel(x, p, attn_pre_g, w_in, cmp_pe_k, cmp_w1_k, cmp_w2_k, cmp_pe_v, cmp_w1_v, cmp_w2_v, sinks, w_o,
           attn_post_g, mlp_pre_g, w_gate_up, conv_w, conv_b, w_down, mlp_post_g, w_ple, w_ple_gate):
    B, S, D = x.shape
    T = B * S
    depth = w_in.shape[0]
    slopes = _alibi_slopes()
    swa_slopes, nsa_slopes = slopes[:SWA_HEADS] * LOG2E, slopes[SWA_HEADS:] * LOG2E
    NS = S // CMP_STRIDE
    kaug = _key_features(S)
    qfeat = _query_features(nsa_slopes)
    ovt = _overlap_t(S, NS)
    swa_bias = _swa_bias(swa_slopes, SWA_WINDOW)
    perm = _head_cols(HEAD_PERM)
    scale = HEAD_DIM ** -0.5 * LOG2E
    tm = 512

    xf = x.reshape(T, D)
    for i in range(depth):
        w = w_in[i]
        q_n = w[:, 0:512][:, perm] * scale
        q_s = w[:, 1304:1816][:, perm] * scale
        gate_cols = jnp.pad(w[:, 1280:1304], ((0, 0), (0, LANES - 24)))
        w_cat = jnp.concatenate([q_n, w[:, 512:1280], q_s, w[:, 1816:2072], gate_cols], axis=1).astype(BF16)
        qn, kc, vc, nkv, swa_qkv, gates = _inproj(xf, attn_pre_g[i][None, :], w_cat, tm)

        wk = _compress_weights(cmp_pe_k[i], cmp_w1_k[i], cmp_w2_k[i])
        wv = _compress_weights(cmp_pe_v[i], cmp_w1_v[i], cmp_w2_v[i])
        seg_w = CMP_STRIDE * GROUPS * HEAD_DIM
        kcmp, vcmp = _compress(kc.reshape(B, NS, seg_w), vc.reshape(B, NS, seg_w), wk, wv)

        o_nsa = _nsa(nsa_slopes, qn.reshape(B, S, 512), gates.reshape(B, S, 128), kcmp, vcmp,
                     nkv.reshape(B, S, 512), kaug, qfeat, ovt, C=256, TK=512)
        o_swa = _swa(sinks[i] * LOG2E, swa_qkv.reshape(B, S, 768), swa_bias, C=2 * SWA_WINDOW)

        wo = w_o[i]
        wgu = w_gate_up[i].astype(BF16)
        xf = _ffn(o_nsa.reshape(T, 512), o_swa.reshape(T, 512), xf, p[i].reshape(T, PLE_DIM),
                  wo[:512][perm].astype(BF16), wo[512:][perm].astype(BF16), attn_post_g[i][None, :],
                  mlp_pre_g[i][None, :], wgu[:, :D_FF], wgu[:, D_FF:], conv_w[i], conv_b[i][None, :],
                  w_down[i].astype(BF16), mlp_post_g[i][None, :], w_ple[i].astype(BF16),
                  w_ple_gate[i].astype(BF16), tm, S)
    return xf.reshape(B, S, D)
```

```python
import functools

import numpy as np
import jax
import jax.numpy as jnp
from jax import lax
from jax.experimental import pallas as pl
from jax.experimental.pallas import tpu as pltpu

F32 = jnp.float32
BF16 = jnp.bfloat16

D_MODEL = 1024
HEAD_DIM = 64
NSA_HEADS = 8
SWA_HEADS = 8
N_HEADS = NSA_HEADS + SWA_HEADS
CMP_LEN = 32
CMP_STRIDE = 16
CMP_HIDDEN = 128
SEL_LEN = 64
N_SEL = 16
NSA_WINDOW = 512
SWA_WINDOW = 128
D_FF = 2816
CONV_W = 3
PLE_DIM = 256
RMS_EPS = 1e-6
NEG_INF = -1e30
FORCE_BONUS = 1e6

LANES = 128
MASK_BIG = 2.0 ** 100
M_INIT = -3.0e38
GROUPS = 2
REP = 4
HEAD_PERM = (0, 4, 1, 5, 2, 6, 3, 7)
N_ALIBI_TERMS = 3
LOG2E = 1.4426950408889634

VMEM_LIMIT = 56 * 1024 * 1024
TOKEN_TILE = 1024
ATTN_TILE = 256


def _rms(x, g):
    return x * lax.rsqrt(jnp.mean(x * x, axis=-1, keepdims=True) + RMS_EPS) * g


def _gelu_tanh(x):
    c = np.sqrt(2.0 / np.pi).astype(np.float32)
    return x * (0.5 * (1.0 + jnp.tanh(c * (x + 0.044715 * (x * x * x)))))


def _dot(a, b):
    return jnp.dot(a, b, preferred_element_type=F32)


def _dot_nt(a, b):
    return lax.dot_general(a, b, (((1,), (1,)), ((), ())), preferred_element_type=F32)


def _rep_lanes(a, n):
    return a if n == 1 else jnp.concatenate([a] * n, axis=1)


def _params(sem):
    return pltpu.CompilerParams(dimension_semantics=sem, vmem_limit_bytes=VMEM_LIMIT)


def _inproj_body(x_ref, g_ref, w_ref, qn_ref, kc_ref, vc_ref, nkv_ref, swa_ref, gates_ref):
    h = _rms(x_ref[...], g_ref[...]).astype(BF16)
    proj = _dot(h, w_ref[...])
    qn_ref[...] = proj[:, 0:512].astype(BF16)
    kc_ref[...] = proj[:, 512:640].astype(BF16)
    vc_ref[...] = proj[:, 640:768].astype(BF16)
    nkv_ref[...] = proj[:, 768:1280].astype(BF16)
    swa_ref[...] = proj[:, 1280:2048].astype(BF16)
    gates_ref[...] = jax.nn.sigmoid(proj[:, 2048:2176])


def _inproj(x2d, g, w_cat, tm):
    T = x2d.shape[0]
    wn = w_cat.shape[1]
    row = lambda n: pl.BlockSpec((tm, n), lambda i: (i, 0))
    full = lambda a: pl.BlockSpec(a.shape, lambda i: (0,) * a.ndim)
    return pl.pallas_call(
        _inproj_body,
        grid=(T // tm,),
        in_specs=[row(D_MODEL), full(g), full(w_cat)],
        out_specs=[row(512), row(128), row(128), row(512), row(768), row(128)],
        out_shape=[
            jax.ShapeDtypeStruct((T, 512), BF16),
            jax.ShapeDtypeStruct((T, 128), BF16),
            jax.ShapeDtypeStruct((T, 128), BF16),
            jax.ShapeDtypeStruct((T, 512), BF16),
            jax.ShapeDtypeStruct((T, 768), BF16),
            jax.ShapeDtypeStruct((T, 128), F32),
        ],
        compiler_params=_params(("parallel",)),
        name="inproj",
    )(x2d, g, w_cat)


def _compress_body(kc_ref, vc_ref,
                   pea_k, peb_k, w1a_k, w1b_k, w2_k,
                   pea_v, peb_v, w1a_v, w1b_v, w2_v,
                   kcmp_ref, vcmp_ref):
    def one(seg_ref, pea, peb, w1a, w1b, w2, out_ref):
        seg = seg_ref[...].astype(F32)
        ns = seg.shape[0]
        ha = _dot((seg + pea[...]).astype(BF16), w1a[...])
        hb = _dot((seg + peb[...]).astype(BF16), w1b[...])
        hid = ha + pltpu.roll(hb, ns - 1, 0)
        act = _gelu_tanh(hid).astype(BF16)
        out_ref[...] = _dot(act, w2[...]).astype(BF16)

    one(kc_ref, pea_k, peb_k, w1a_k, w1b_k, w2_k, kcmp_ref)
    one(vc_ref, pea_v, peb_v, w1a_v, w1b_v, w2_v, vcmp_ref)


def _compress_weights(pe, w1, w2):
    half = CMP_LEN // 2
    eye = jnp.eye(GROUPS, dtype=F32)
    w1r = w1.reshape(CMP_LEN, HEAD_DIM, CMP_HIDDEN)

    def expand(w):
        return jnp.einsum("ldj,pg->lpdgj", w, eye).reshape(half * GROUPS * HEAD_DIM, GROUPS * CMP_HIDDEN)

    def pe_row(p):
        return jnp.broadcast_to(p[:, None, :], (half, GROUPS, HEAD_DIM)).reshape(1, half * GROUPS * HEAD_DIM)

    w2e = jnp.einsum("jd,pg->pjgd", w2, eye).reshape(GROUPS * CMP_HIDDEN, GROUPS * HEAD_DIM)
    return (pe_row(pe[:half]), pe_row(pe[half:]),
            expand(w1r[:half]).astype(BF16), expand(w1r[half:]).astype(BF16), w2e.astype(BF16))


def _compress(kc_seg, vc_seg, wk, wv):
    B, NS, WD = kc_seg.shape
    seg = pl.BlockSpec((None, NS, WD), lambda b: (b, 0, 0))
    full = lambda a: pl.BlockSpec(a.shape, lambda b: (0,) * a.ndim)
    out = pl.BlockSpec((None, NS, 128), lambda b: (b, 0, 0))
    return pl.pallas_call(
        _compress_body,
        grid=(B,),
        in_specs=[seg, seg] + [full(a) for a in wk] + [full(a) for a in wv],
        out_specs=[out, out],
        out_shape=[jax.ShapeDtypeStruct((B, NS, 128), BF16)] * 2,
        compiler_params=_params(("parallel",)),
        name="compress",
    )(kc_seg, vc_seg, *wk, *wv)


def _group_rows(q_all, g, C):
    lane = lax.broadcasted_iota(jnp.int32, (C, LANES), 1)
    keep = (lane < HEAD_DIM) if g == 0 else (lane >= HEAD_DIM)
    zero = jnp.zeros((C, LANES), q_all.dtype)
    return jnp.concatenate(
        [jnp.where(keep, q_all[:, r * LANES:(r + 1) * LANES], zero) for r in range(REP)], axis=0)


def _nsa_body(slopes_ref, qn_ref, gates_ref, kcmp_ref, vcmp_ref, ksl_ref, vsl_ref, kwn_ref, vwn_ref,
              kaug_ref, qfeat_ref, ovt_ref,
              sinks_ref, sq_ref, skp_ref, skc_ref, svp_ref, svc_ref, sbias_ref,
              out_ref, swa_out_ref,
              qa_ref, m_ref, acc_ref, win_ref, part_ref, imp_ref, rank_ref, *, C, TK, W, NSB, NC, n_top):
    t0 = pl.program_id(1) * C
    _swa_body(sinks_ref, sq_ref, skp_ref, skc_ref, svp_ref, svc_ref, sbias_ref, swa_out_ref, C=C)
    q_all = qn_ref[...]
    qg = [_group_rows(q_all, g, C) for g in range(GROUPS)]
    gates = gates_ref[...]

    def softmax_rows(s):
        e = jnp.exp2(s - jnp.max(s, axis=-1, keepdims=True))
        return e * (1.0 / jnp.sum(e, axis=-1, keepdims=True))

    assert C <= W
    rowi = lax.broadcasted_iota(jnp.int32, (C, C), 0)
    coli = lax.broadcasted_iota(jnp.int32, (C, C), 1)
    causal_bias = jnp.where(coli <= rowi, 0.0, NEG_INF)
    pstart = pl.multiple_of(jnp.maximum(t0 - W, 0), C)
    ppos = pstart + lax.broadcasted_iota(jnp.int32, (C, W), 1)
    pdist = (t0 + lax.broadcasted_iota(jnp.int32, (C, W), 0)) - ppos
    past_bias = jnp.where((pdist < W) & (ppos < t0), 0.0, NEG_INF)

    def win_operands():
        td = pl.multiple_of(t0, C)
        kd = jnp.concatenate([kwn_ref[pl.ds(td, C), :], kaug_ref[pl.ds(td, C), :]], axis=1)
        kp = jnp.concatenate([kwn_ref[pl.ds(pstart, W), :], kaug_ref[pl.ds(pstart, W), :]], axis=1)
        vals = jnp.concatenate([vwn_ref[pl.ds(pstart, W), :], vwn_ref[pl.ds(td, C), :]], axis=0)
        return kd, kp, jnp.concatenate([vals, jnp.ones((W + C, LANES), BF16)], axis=1)

    def win_queries(g):
        feats = jnp.concatenate(
            [jnp.broadcast_to(qfeat_ref[g * REP + r:g * REP + r + 1, :], (C, LANES)) for r in range(REP)], axis=0)
        return jnp.concatenate([qg[g], feats.astype(BF16)], axis=1)

    kd, kp, vw = win_operands()
    for g in range(GROUPS):
        qw = win_queries(g)
        sd = _dot_nt(qw, kd)
        sp = _dot_nt(qw, kp)
        probs = []
        for r in range(REP):
            rows = slice(r * C, (r + 1) * C)
            srd = sd[rows] + causal_bias
            m = jnp.max(srd, axis=-1, keepdims=True)
            probs.append(jnp.concatenate([jnp.exp2(sp[rows] + past_bias - m), jnp.exp2(srd - m)], axis=1).astype(BF16))
        win_ref[g] = _dot(jnp.concatenate(probs, axis=0), vw)

    tq_c = t0 + lax.broadcasted_iota(jnp.int32, (C, NC), 0)
    cend = lax.broadcasted_iota(jnp.int32, (C, NC), 1) * CMP_STRIDE + (CMP_LEN - 1)
    valid_c = cend <= tq_c
    cend_rel = (cend - t0).astype(F32)
    for g in range(GROUPS):
        sc = _dot_nt(qg[g], kcmp_ref[...])
        psum = jnp.zeros((C, NC), F32)
        probs = []
        for r in range(REP):
            s = sc[r * C:(r + 1) * C] + slopes_ref[g * REP + r] * cend_rel
            p = jnp.where(valid_c, softmax_rows(jnp.where(valid_c, s, NEG_INF)), 0.0)
            psum = psum + p
            probs.append(p.astype(BF16))
        o_cmp = _dot(jnp.concatenate(probs, axis=0), vcmp_ref[...])
        for r in range(REP):
            hcol = 3 * (g * REP + r)
            part_ref[g, r * C:(r + 1) * C, :] = gates[:, hcol:hcol + 1] * o_cmp[r * C:(r + 1) * C]

        ovt = ovt_ref[...]
        p_hi = psum.astype(BF16)
        r1 = psum - p_hi.astype(F32)
        p_mid = r1.astype(BF16)
        p_lo = (r1 - p_mid.astype(F32)).astype(BF16)
        imp = _dot_nt(ovt, p_hi) + _dot_nt(ovt, p_mid) + _dot_nt(ovt, p_lo)

        jblk = lax.broadcasted_iota(jnp.int32, (NSB, C), 0)
        jt = lax.shift_right_logical(t0 + lax.broadcasted_iota(jnp.int32, (NSB, C), 1), 6)
        forced = (jblk == 0) | (jblk == jt) | (jblk == jt - 1)
        visible = jblk <= jt
        imp = jnp.where(visible, jnp.where(forced, FORCE_BONUS, imp), NEG_INF)
        imp_ref[...] = imp
        rank_ref[...] = jnp.zeros(rank_ref.shape, F32)
        jrow = lax.broadcasted_iota(jnp.int32, (8, LANES), 0)

        def rank_pass(nblk):
            nv = nblk // 8
            for c0 in range(0, C, LANES):
                cols = slice(c0, c0 + LANES)
                blks = [imp_ref[8 * v:8 * v + 8, cols] for v in range(nv)]
                cnt = [jnp.zeros((8, LANES), F32) for _ in range(nv)]
                for i in range(nblk):
                    row = jnp.broadcast_to(imp_ref[i:i + 1, cols], (8, LANES))
                    for v in range(nv):
                        if i < 8 * v:
                            before = row >= blks[v]
                        elif i >= 8 * v + 8:
                            before = row > blks[v]
                        else:
                            before = (row > blks[v]) | ((row == blks[v]) & (jrow > i - 8 * v))
                        cnt[v] = cnt[v] + jnp.where(before, 1.0, 0.0)
                rank_ref[0:nblk, cols] = jnp.concatenate(cnt, axis=0)

        n_seen = (t0 + C) // SEL_LEN
        half = NSB // 2
        if half > n_top and half % 8 == 0:
            pl.when((n_seen > n_top) & (n_seen <= half))(functools.partial(rank_pass, half))
            pl.when(n_seen > half)(functools.partial(rank_pass, NSB))
        else:
            pl.when(n_seen > n_top)(functools.partial(rank_pass, NSB))

        selb_chunks = []
        for c0 in range(0, C, LANES):
            rank = rank_ref[:, c0:c0 + LANES]
            visible_c = (lax.broadcasted_iota(jnp.int32, (NSB, LANES), 0)
                         <= lax.shift_right_logical(t0 + c0 + lax.broadcasted_iota(jnp.int32, (NSB, LANES), 1), 6))
            sel_t = jnp.where(visible_c & (rank < n_top), 0.0, -MASK_BIG)
            sel_t = jnp.concatenate([sel_t, jnp.zeros((LANES - NSB, LANES), F32)], axis=0)
            selb_chunks.append(sel_t.T)
        selb = jnp.concatenate(selb_chunks, axis=0)
        for r in range(REP):
            feat = qfeat_ref[g * REP + r:g * REP + r + 1, :]
            qa_ref[g, r * C:(r + 1) * C, 0:LANES] = qg[g][r * C:(r + 1) * C]
            qa_ref[g, r * C:(r + 1) * C, LANES:2 * LANES] = (selb + feat).astype(BF16)

    assert TK == 2 * C
    tq = t0 + lax.broadcasted_iota(jnp.int32, (C, C), 0)
    kcol = lax.broadcasted_iota(jnp.int32, (C, C), 1)

    def load_tile(k0, width):
        kk = jnp.concatenate([ksl_ref[pl.ds(k0, width), :], kaug_ref[pl.ds(k0, width), :]], axis=1)
        vv = jnp.concatenate([vsl_ref[pl.ds(k0, width), :], jnp.ones((width, LANES), BF16)], axis=1)
        return kk, vv

    kk, vv = load_tile(pl.multiple_of(t0, C), C)
    cb = jnp.where(kcol <= lax.broadcasted_iota(jnp.int32, (C, C), 0), 0.0, NEG_INF)
    for g in range(GROUPS):
        s = _dot_nt(qa_ref[g], kk)
        probs = []
        for r in range(REP):
            rows = slice(r * C, (r + 1) * C)
            sr = s[rows] + cb
            m = jnp.max(sr, axis=-1, keepdims=True)
            m_ref[g, rows, :] = jnp.broadcast_to(m, (C, LANES))
            probs.append(jnp.exp2(sr - m).astype(BF16))
        acc_ref[0, g] = _dot(jnp.concatenate(probs, axis=0), vv)
    acc_ref[1] = jnp.zeros(acc_ref.shape[1:], F32)

    def past_tile(k0, width):
        kk, vv = load_tile(k0, width)
        outs = []
        for g in range(GROUPS):
            s = _dot_nt(qa_ref[g], kk)
            probs = [jnp.exp2(s[r * C:(r + 1) * C] - _rep_lanes(m_ref[g, r * C:(r + 1) * C, :], width // LANES))
                     .astype(BF16) for r in range(REP)]
            outs.append(_dot(jnp.concatenate(probs, axis=0), vv))
        return outs

    n_full = t0 // TK

    def pair_body(k, carry):
        a = past_tile(pl.multiple_of(2 * k * TK, TK), TK)
        b = past_tile(pl.multiple_of((2 * k + 1) * TK, TK), TK)
        for g in range(GROUPS):
            acc_ref[0, g] += a[g]
            acc_ref[1, g] += b[g]
        return carry

    lax.fori_loop(0, n_full // 2, pair_body, 0)

    @pl.when(n_full % 2 == 1)
    def _():
        a = past_tile(pl.multiple_of((n_full - 1) * TK, TK), TK)
        for g in range(GROUPS):
            acc_ref[0, g] += a[g]

    @pl.when(t0 % TK == C)
    def _():
        a = past_tile(pl.multiple_of(t0 - C, C), C)
        for g in range(GROUPS):
            acc_ref[1, g] += a[g]

    def tree_sum(parts):
        while len(parts) > 1:
            parts = [parts[i] + parts[i + 1] for i in range(0, len(parts), 2)]
        return parts[0]

    def combine():
        lane = lax.broadcasted_iota(jnp.int32, (C, LANES), 1)
        seen = []
        for r in range(REP):
            chunk = None
            for g in range(GROUPS):
                hcol = 3 * (g * REP + r)
                rows = slice(r * C, (r + 1) * C)
                sel = acc_ref[0, g, rows, :] + acc_ref[1, g, rows, :]
                win = win_ref[g, rows, :]
                seen += [sel, win]
                o = (part_ref[g, rows, :]
                     + gates[:, hcol + 1:hcol + 2] * (sel[:, :LANES] * (1.0 / sel[:, LANES:]))
                     + gates[:, hcol + 2:hcol + 3] * (win[:, :LANES] * (1.0 / win[:, LANES:])))
                chunk = o if g == 0 else jnp.where(lane < HEAD_DIM, chunk, o)
            out_ref[:, r * LANES:(r + 1) * LANES] = chunk.astype(BF16)
        return jnp.sum(tree_sum(seen))

    @pl.when(jnp.logical_not(jnp.isfinite(combine())))
    def _():
        kd, kp, vw = win_operands()
        for g in range(GROUPS):
            qw = win_queries(g)
            sd = _dot_nt(qw, kd)
            sp = _dot_nt(qw, kp)
            probs = []
            for r in range(REP):
                rows = slice(r * C, (r + 1) * C)
                sr = jnp.concatenate([sp[rows] + past_bias, sd[rows] + causal_bias], axis=1)
                probs.append(jnp.exp2(sr - jnp.max(sr, axis=-1, keepdims=True)).astype(BF16))
            win_ref[g] = _dot(jnp.concatenate(probs, axis=0), vw)

        m_ref[...] = jnp.full(m_ref.shape, M_INIT, F32)
        acc_ref[...] = jnp.zeros(acc_ref.shape, F32)

        def online_tile(kt, carry):
            k0 = pl.multiple_of(kt * C, C)
            kk, vv = load_tile(k0, C)
            cb = jnp.where(k0 + kcol <= tq, 0.0, NEG_INF)
            for g in range(GROUPS):
                s = _dot_nt(qa_ref[g], kk)
                probs = []
                for r in range(REP):
                    rows = slice(r * C, (r + 1) * C)
                    sr = s[rows] + cb
                    m_old = m_ref[g, rows, :]
                    m_new = jnp.maximum(m_old, jnp.max(sr, axis=-1, keepdims=True))
                    acc_ref[0, g, rows, :] = _rep_lanes(jnp.exp2(m_old - m_new), 2) * acc_ref[0, g, rows, :]
                    m_ref[g, rows, :] = m_new
                    probs.append(jnp.exp2(sr - _rep_lanes(m_new, C // LANES)).astype(BF16))
                acc_ref[0, g] += _dot(jnp.concatenate(probs, axis=0), vv)
            return carry

        lax.fori_loop(0, t0 // C + 1, online_tile, 0)
        combine()


def _attention(slopes, qn, gates, kcmp, vcmp, nkv, kaug, qfeat, ovt, sinks_l2, swa_qkv, swa_bias, *, C, TK):
    B, S, _ = qn.shape
    NC = kcmp.shape[1]
    NSB = S // SEL_LEN
    n_top = min(N_SEL, NSB)
    W = NSA_WINDOW
    body = functools.partial(_nsa_body, C=C, TK=TK, W=W, NSB=NSB, NC=NC, n_top=n_top)
    qblk = lambda n: pl.BlockSpec((None, C, n), lambda b, i: (b, i, 0))
    per_b = lambda n: pl.BlockSpec((None, n, 128), lambda b, i: (b, 0, 0))
    nkv_col = lambda j: pl.BlockSpec((None, S, 128), lambda b, i, j=j: (b, 0, j))
    full = lambda a: pl.BlockSpec(a.shape, lambda b, i: (0,) * a.ndim)
    sub = C // SWA_WINDOW
    swa_cur = lambda j: pl.BlockSpec((None, C, 128), lambda b, i, j=j: (b, i, j))
    swa_prev = lambda j: pl.BlockSpec((None, SWA_WINDOW, 128), lambda b, i, j=j: (b, jnp.maximum(i * sub - 1, 0), j))
    smem = pl.BlockSpec(memory_space=pltpu.SMEM)
    return pl.pallas_call(
        body,
        grid=(B, S // C),
        in_specs=[smem, qblk(512), qblk(128), per_b(NC), per_b(NC),
                  nkv_col(0), nkv_col(1), nkv_col(2), nkv_col(3),
                  full(kaug), full(qfeat), full(ovt),
                  smem, qblk(512), swa_prev(4), swa_cur(4), swa_prev(5), swa_cur(5), full(swa_bias)],
        out_specs=[qblk(512), qblk(512)],
        out_shape=[jax.ShapeDtypeStruct((B, S, 512), BF16)] * 2,
        scratch_shapes=[
            pltpu.VMEM((GROUPS, REP * C, 2 * LANES), BF16),
            pltpu.VMEM((GROUPS, REP * C, LANES), F32),
            pltpu.VMEM((2, GROUPS, REP * C, 2 * LANES), F32),
            pltpu.VMEM((GROUPS, REP * C, 2 * LANES), F32),
            pltpu.VMEM((GROUPS, REP * C, LANES), F32),
            pltpu.VMEM((NSB, C), F32),
            pltpu.VMEM((NSB, C), F32),
        ],
        compiler_params=_params(("parallel", "parallel")),
        name="attention",
    )(slopes, qn, gates, kcmp, vcmp, nkv, nkv, nkv, nkv, kaug, qfeat, ovt,
      sinks_l2, swa_qkv, swa_qkv, swa_qkv, swa_qkv, swa_qkv, swa_bias)


def _swa_body(sinks_ref, q_ref, kp_ref, kc_ref, vp_ref, vc_ref, bias_ref, out_ref, *, C):
    WB = SWA_WINDOW
    at_start = jnp.where(pl.program_id(1) == 0, 1, 0)
    kall = jnp.concatenate([kp_ref[...], kc_ref[...]], axis=0)
    vall = jnp.concatenate([vp_ref[...], vc_ref[...]], axis=0)
    vall = jnp.concatenate([vall, jnp.ones((WB + C, LANES), BF16)], axis=1)
    lane = lax.broadcasted_iota(jnp.int32, (WB, LANES), 1)
    for sb in range(C // WB):
        q_sb = q_ref[sb * WB:(sb + 1) * WB, :]
        kk = kall[sb * WB:(sb + 2) * WB]
        vv = vall[sb * WB:(sb + 2) * WB]
        table = at_start if sb == 0 else 0
        outs = []
        for g in range(GROUPS):
            s_all = _dot_nt(_group_rows(q_sb, g, WB), kk)
            probs, sink_terms = [], []
            for r in range(REP):
                h = g * REP + r
                s = s_all[r * WB:(r + 1) * WB] + bias_ref[table, h]
                sink = sinks_ref[h]
                m = jnp.maximum(jnp.max(s, axis=-1, keepdims=True), sink)
                probs.append(jnp.exp2(s - m).astype(BF16))
                sink_terms.append(jnp.exp2(sink - m))
            o = _dot(jnp.concatenate(probs, axis=0), vv)
            outs.append([o[r * WB:(r + 1) * WB, :LANES] * (1.0 / (o[r * WB:(r + 1) * WB, LANES:] + sink_terms[r]))
                         for r in range(REP)])
        for r in range(REP):
            chunk = jnp.where(lane < HEAD_DIM, outs[0][r], outs[1][r])
            out_ref[sb * WB:(sb + 1) * WB, r * LANES:(r + 1) * LANES] = chunk.astype(BF16)


def _swa_bias(slopes_l2, C):
    qi = np.arange(C)[:, None]
    kj = np.arange(2 * C)[None, :]
    dist = C + qi - kj
    valid = (dist >= 0) & (dist < SWA_WINDOW)
    tab = -slopes_l2[:, None, None] * jnp.asarray(dist, F32)[None]
    return jnp.stack([jnp.where(valid[None], tab, NEG_INF), jnp.where((valid & (kj >= C))[None], tab, NEG_INF)])


CONV_TAIL = 8
FFN_CHUNK = 256


def _ffn_body(on_ref, os_ref, x_ref, p_ref, wn_ref, ws_ref, g0_ref, g1_ref, wg_ref, wu_ref, cw_ref, cb_ref,
              wd_ref, g2_ref, wple_ref, wpg_ref, out_ref, act_ref, tail_ref, *, tm, tiles_per_seq):
    @pl.when(pl.program_id(0) % tiles_per_seq == 0)
    def _():
        tail_ref[...] = jnp.zeros(tail_ref.shape, F32)

    mix = _dot(on_ref[...], wn_ref[...]) + _dot(os_ref[...], ws_ref[...])
    x1 = x_ref[...] + _rms(mix, g0_ref[...])
    h = _rms(x1, g1_ref[...]).astype(BF16)
    for c in range(D_FF // FFN_CHUNK):
        cols = slice(c * FFN_CHUNK, (c + 1) * FFN_CHUNK)
        a = _dot(h, wg_ref[:, cols])
        u = _dot(h, wu_ref[:, cols])
        ae = jnp.concatenate([tail_ref[:, cols], a], axis=0)
        tail_ref[:, cols] = a[tm - CONV_TAIL:]
        conv = (cw_ref[2:3, cols] * a
                + cw_ref[1:2, cols] * pltpu.roll(ae, 1, 0)[CONV_TAIL:]
                + cw_ref[0:1, cols] * pltpu.roll(ae, 2, 0)[CONV_TAIL:]
                + cb_ref[:, cols])
        act_ref[:, cols] = (_gelu_tanh(conv) * u).astype(BF16)
    y = _dot(act_ref[...], wd_ref[...])
    x2 = x1 + _rms(y, g2_ref[...])
    e = _dot(p_ref[...].astype(BF16), wple_ref[...])
    gate = jax.nn.sigmoid(_dot(x2.astype(BF16), wpg_ref[...]))
    out_ref[...] = x2 + e * gate


def _ffn(o_nsa, o_swa, x2d, p2d, wn, ws, g0, g1, wg, wu, cw, cb, wd, g2, wple, wpg, tm, S):
    T = x2d.shape[0]
    body = functools.partial(_ffn_body, tm=tm, tiles_per_seq=S // tm)
    row = lambda n: pl.BlockSpec((tm, n), lambda i: (i, 0))
    const = lambda a: pl.BlockSpec(a.shape, lambda i: (0,) * a.ndim, pipeline_mode=pl.Buffered(1))
    weights = (wn, ws, g0, g1, wg, wu, cw, cb, wd, g2, wple, wpg)
    return pl.pallas_call(
        body,
        grid=(T // tm,),
        in_specs=[row(512), row(512), row(D_MODEL), row(PLE_DIM)] + [const(a) for a in weights],
        out_specs=row(D_MODEL),
        out_shape=jax.ShapeDtypeStruct((T, D_MODEL), F32),
        scratch_shapes=[pltpu.VMEM((tm, D_FF), BF16), pltpu.VMEM((CONV_TAIL, D_FF), F32)],
        compiler_params=_params(("arbitrary",)),
        name="ffn",
    )(o_nsa, o_swa, x2d, p2d, *weights)


def _alibi_slopes():
    h = jnp.arange(N_HEADS, dtype=F32)
    return jnp.exp2(-8.0 * (h + 1.0) / N_HEADS)


def _key_features(S):
    pos = np.arange(S)
    blk, off = pos // SEL_LEN, pos % SEL_LEN
    feat = np.zeros((S, LANES), np.float32)
    feat[pos, blk] = 1.0
    base = S // SEL_LEN
    assert base <= HEAD_DIM
    for k in range(N_ALIBI_TERMS):
        feat[:, HEAD_DIM + k] = blk * SEL_LEN
        feat[:, HEAD_DIM + N_ALIBI_TERMS + k] = off
    return jnp.asarray(feat, BF16)


def _query_features(slopes):
    terms, rest = [], slopes
    for _ in range(N_ALIBI_TERMS):
        t = rest.astype(BF16).astype(F32)
        terms.append(t)
        rest = rest - t
    tt = jnp.stack(terms, axis=1)
    feat = jnp.zeros((slopes.shape[0], LANES), F32)
    feat = feat.at[:, HEAD_DIM:HEAD_DIM + N_ALIBI_TERMS].set(tt)
    feat = feat.at[:, HEAD_DIM + N_ALIBI_TERMS:HEAD_DIM + 2 * N_ALIBI_TERMS].set(tt)
    return feat


def _overlap_t(S, NC):
    n = np.arange(NC)[None, :]
    j = np.arange(S // SEL_LEN)[:, None]
    cs, ss = n * CMP_STRIDE, j * SEL_LEN
    ov = (cs < ss + SEL_LEN) & (cs + CMP_LEN - 1 >= ss)
    return jnp.asarray(ov.astype(np.float32), BF16)


def _head_cols(perm):
    return np.concatenate([np.arange(h * HEAD_DIM, (h + 1) * HEAD_DIM) for h in perm])


def kernel(x, p, attn_pre_g, w_in, cmp_pe_k, cmp_w1_k, cmp_w2_k, cmp_pe_v, cmp_w1_v, cmp_w2_v, sinks, w_o,
           attn_post_g, mlp_pre_g, w_gate_up, conv_w, conv_b, w_down, mlp_post_g, w_ple, w_ple_gate):
    B, S, D = x.shape
    T = B * S
    depth = w_in.shape[0]
    slopes = _alibi_slopes()
    swa_slopes, nsa_slopes = slopes[:SWA_HEADS] * LOG2E, slopes[SWA_HEADS:] * LOG2E
    NS = S // CMP_STRIDE
    kaug = _key_features(S)
    qfeat = _query_features(nsa_slopes)
    ovt = _overlap_t(S, NS)
    swa_bias = _swa_bias(swa_slopes, SWA_WINDOW)
    perm = _head_cols(HEAD_PERM)
    scale = HEAD_DIM ** -0.5 * LOG2E
    tm = TOKEN_TILE

    xf = x.reshape(T, D)
    for i in range(depth):
        w = w_in[i]
        q_n = w[:, 0:512][:, perm] * scale
        q_s = w[:, 1304:1816][:, perm] * scale
        gate_cols = jnp.pad(w[:, 1280:1304], ((0, 0), (0, LANES - 24)))
        w_cat = jnp.concatenate([q_n, w[:, 512:1280], q_s, w[:, 1816:2072], gate_cols], axis=1).astype(BF16)
        qn, kc, vc, nkv, swa_qkv, gates = _inproj(xf, attn_pre_g[i][None, :], w_cat, tm)

        wk = _compress_weights(cmp_pe_k[i], cmp_w1_k[i], cmp_w2_k[i])
        wv = _compress_weights(cmp_pe_v[i], cmp_w1_v[i], cmp_w2_v[i])
        seg_w = CMP_STRIDE * GROUPS * HEAD_DIM
        kcmp, vcmp = _compress(kc.reshape(B, NS, seg_w), vc.reshape(B, NS, seg_w), wk, wv)

        o_nsa, o_swa = _attention(nsa_slopes, qn.reshape(B, S, 512), gates.reshape(B, S, 128), kcmp, vcmp,
                                  nkv.reshape(B, S, 512), kaug, qfeat, ovt, sinks[i] * LOG2E,
                                  swa_qkv.reshape(B, S, 768), swa_bias, C=ATTN_TILE, TK=2 * ATTN_TILE)

        wo = w_o[i]
        wgu = w_gate_up[i].astype(BF16)
        xf = _ffn(o_nsa.reshape(T, 512), o_swa.reshape(T, 512), xf, p[i].reshape(T, PLE_DIM),
                  wo[:512][perm].astype(BF16), wo[512:][perm].astype(BF16), attn_post_g[i][None, :],
                  mlp_pre_g[i][None, :], wgu[:, :D_FF], wgu[:, D_FF:], conv_w[i], conv_b[i][None, :],
                  w_down[i].astype(BF16), mlp_post_g[i][None, :], w_ple[i].astype(BF16),
                  w_ple_gate[i].astype(BF16), tm, S)
    return xf.reshape(B, S, D)
```

```python
import functools

import numpy as np
import jax
import jax.numpy as jnp
from jax import lax
from jax.experimental import pallas as pl
from jax.experimental.pallas import tpu as pltpu

F32 = jnp.float32
BF16 = jnp.bfloat16

D_MODEL = 1024
HEAD_DIM = 64
NSA_HEADS = 8
SWA_HEADS = 8
N_HEADS = NSA_HEADS + SWA_HEADS
CMP_LEN = 32
CMP_STRIDE = 16
CMP_HIDDEN = 128
SEL_LEN = 64
N_SEL = 16
NSA_WINDOW = 512
SWA_WINDOW = 128
D_FF = 2816
CONV_W = 3
PLE_DIM = 256
RMS_EPS = 1e-6
NEG_INF = -1e30
FORCE_BONUS = 1e6

LANES = 128
MASK_BIG = 2.0 ** 100
M_INIT = -3.0e38
GROUPS = 2
REP = 4
HEAD_PERM = (0, 4, 1, 5, 2, 6, 3, 7)
N_ALIBI_TERMS = 3
LOG2E = 1.4426950408889634

VMEM_LIMIT = 56 * 1024 * 1024
TOKEN_TILE = 1024
ATTN_TILE = 256


def _rms(x, g):
    return x * lax.rsqrt(jnp.mean(x * x, axis=-1, keepdims=True) + RMS_EPS) * g


def _gelu_tanh(x):
    c = np.sqrt(2.0 / np.pi).astype(np.float32)
    return x * (0.5 * (1.0 + jnp.tanh(c * (x + 0.044715 * (x * x * x)))))


def _dot(a, b):
    return jnp.dot(a, b, preferred_element_type=F32)


def _dot_nt(a, b):
    return lax.dot_general(a, b, (((1,), (1,)), ((), ())), preferred_element_type=F32)


def _rep_lanes(a, n):
    return a if n == 1 else jnp.concatenate([a] * n, axis=1)


def _params(sem):
    return pltpu.CompilerParams(dimension_semantics=sem, vmem_limit_bytes=VMEM_LIMIT)


def _inproj_body(x_ref, g_ref, w_ref, qn_ref, kc_ref, vc_ref, nkv_ref, swa_ref, gates_ref):
    h = _rms(x_ref[...], g_ref[...]).astype(BF16)
    proj = _dot(h, w_ref[...])
    qn_ref[...] = proj[:, 0:512].astype(BF16)
    kc_ref[...] = proj[:, 512:640].astype(BF16)
    vc_ref[...] = proj[:, 640:768].astype(BF16)
    nkv_ref[...] = proj[:, 768:1280].astype(BF16)
    swa_ref[...] = proj[:, 1280:2048].astype(BF16)
    gates_ref[...] = jax.nn.sigmoid(proj[:, 2048:2176])


def _inproj(x2d, g, w_cat, tm):
    T = x2d.shape[0]
    wn = w_cat.shape[1]
    row = lambda n: pl.BlockSpec((tm, n), lambda i: (i, 0))
    full = lambda a: pl.BlockSpec(a.shape, lambda i: (0,) * a.ndim)
    return pl.pallas_call(
        _inproj_body,
        grid=(T // tm,),
        in_specs=[row(D_MODEL), full(g), full(w_cat)],
        out_specs=[row(512), row(128), row(128), row(512), row(768), row(128)],
        out_shape=[
            jax.ShapeDtypeStruct((T, 512), BF16),
            jax.ShapeDtypeStruct((T, 128), BF16),
            jax.ShapeDtypeStruct((T, 128), BF16),
            jax.ShapeDtypeStruct((T, 512), BF16),
            jax.ShapeDtypeStruct((T, 768), BF16),
            jax.ShapeDtypeStruct((T, 128), F32),
        ],
        compiler_params=_params(("parallel",)),
        name="inproj",
    )(x2d, g, w_cat)


def _compress_body(kc_ref, vc_ref,
                   pea_k, peb_k, w1a_k, w1b_k, w2_k,
                   pea_v, peb_v, w1a_v, w1b_v, w2_v,
                   kcmp_ref, vcmp_ref):
    def one(seg_ref, pea, peb, w1a, w1b, w2, out_ref):
        seg = seg_ref[...].astype(F32)
        ns = seg.shape[0]
        ha = _dot((seg + pea[...]).astype(BF16), w1a[...])
        hb = _dot((seg + peb[...]).astype(BF16), w1b[...])
        hid = ha + pltpu.roll(hb, ns - 1, 0)
        act = _gelu_tanh(hid).astype(BF16)
        out_ref[...] = _dot(act, w2[...]).astype(BF16)

    one(kc_ref, pea_k, peb_k, w1a_k, w1b_k, w2_k, kcmp_ref)
    one(vc_ref, pea_v, peb_v, w1a_v, w1b_v, w2_v, vcmp_ref)


def _compress_weights(pe, w1, w2):
    half = CMP_LEN // 2
    eye = jnp.eye(GROUPS, dtype=F32)
    w1r = w1.reshape(CMP_LEN, HEAD_DIM, CMP_HIDDEN)

    def expand(w):
        return jnp.einsum("ldj,pg->lpdgj", w, eye).reshape(half * GROUPS * HEAD_DIM, GROUPS * CMP_HIDDEN)

    def pe_row(p):
        return jnp.broadcast_to(p[:, None, :], (half, GROUPS, HEAD_DIM)).reshape(1, half * GROUPS * HEAD_DIM)

    w2e = jnp.einsum("jd,pg->pjgd", w2, eye).reshape(GROUPS * CMP_HIDDEN, GROUPS * HEAD_DIM)
    return (pe_row(pe[:half]), pe_row(pe[half:]),
            expand(w1r[:half]).astype(BF16), expand(w1r[half:]).astype(BF16), w2e.astype(BF16))


def _compress(kc_seg, vc_seg, wk, wv):
    B, NS, WD = kc_seg.shape
    seg = pl.BlockSpec((None, NS, WD), lambda b: (b, 0, 0))
    full = lambda a: pl.BlockSpec(a.shape, lambda b: (0,) * a.ndim)
    out = pl.BlockSpec((None, NS, 128), lambda b: (b, 0, 0))
    return pl.pallas_call(
        _compress_body,
        grid=(B,),
        in_specs=[seg, seg] + [full(a) for a in wk] + [full(a) for a in wv],
        out_specs=[out, out],
        out_shape=[jax.ShapeDtypeStruct((B, NS, 128), BF16)] * 2,
        compiler_params=_params(("parallel",)),
        name="compress",
    )(kc_seg, vc_seg, *wk, *wv)


def _group_rows(q_all, g, C):
    lane = lax.broadcasted_iota(jnp.int32, (C, LANES), 1)
    keep = (lane < HEAD_DIM) if g == 0 else (lane >= HEAD_DIM)
    zero = jnp.zeros((C, LANES), q_all.dtype)
    return jnp.concatenate(
        [jnp.where(keep, q_all[:, r * LANES:(r + 1) * LANES], zero) for r in range(REP)], axis=0)


def _nsa_body(qn_ref, gates_ref, kcmp_ref, vcmp_ref, ksl_ref, vsl_ref, kwn_ref, vwn_ref,
              kaug_ref, caug_ref, qfeat_ref, ovt_ref,
              sinks_ref, sq_ref, skp_ref, skc_ref, svp_ref, svc_ref, sbias_ref,
              out_ref, swa_out_ref,
              qa_ref, m_ref, acc_ref, win_ref, part_ref, imp_ref, rank_ref, *, C, TK, W, NSB, NC, n_top):
    t0 = pl.program_id(1) * C
    _swa_body(sinks_ref, sq_ref, skp_ref, skc_ref, svp_ref, svc_ref, sbias_ref, swa_out_ref, C=C)
    q_all = qn_ref[...]
    qg = [_group_rows(q_all, g, C) for g in range(GROUPS)]
    gates = gates_ref[...]

    assert C <= W
    rowi = lax.broadcasted_iota(jnp.int32, (C, C), 0)
    coli = lax.broadcasted_iota(jnp.int32, (C, C), 1)
    causal_bias = jnp.where(coli <= rowi, 0.0, NEG_INF)
    pstart = pl.multiple_of(jnp.maximum(t0 - W, 0), C)
    ppos = pstart + lax.broadcasted_iota(jnp.int32, (C, W), 1)
    pdist = (t0 + lax.broadcasted_iota(jnp.int32, (C, W), 0)) - ppos
    past_bias = jnp.where((pdist < W) & (ppos < t0), 0.0, NEG_INF)

    def win_operands():
        td = pl.multiple_of(t0, C)
        kd = jnp.concatenate([kwn_ref[pl.ds(td, C), :], kaug_ref[pl.ds(td, C), :]], axis=1)
        kp = jnp.concatenate([kwn_ref[pl.ds(pstart, W), :], kaug_ref[pl.ds(pstart, W), :]], axis=1)
        vals = jnp.concatenate([vwn_ref[pl.ds(pstart, W), :], vwn_ref[pl.ds(td, C), :]], axis=0)
        return kd, kp, jnp.concatenate([vals, jnp.ones((W + C, LANES), BF16)], axis=1)

    def win_queries(g):
        feats = jnp.concatenate(
            [jnp.broadcast_to(qfeat_ref[g * REP + r:g * REP + r + 1, :], (C, LANES)) for r in range(REP)], axis=0)
        return jnp.concatenate([qg[g], feats.astype(BF16)], axis=1)

    kd, kp, vw = win_operands()
    for g in range(GROUPS):
        qw = win_queries(g)
        sd = _dot_nt(qw, kd)
        sp = _dot_nt(qw, kp)
        probs = []
        for r in range(REP):
            rows = slice(r * C, (r + 1) * C)
            srd = sd[rows] + causal_bias
            m = jnp.max(srd, axis=-1, keepdims=True)
            probs.append(jnp.concatenate([jnp.exp2(sp[rows] + past_bias - m), jnp.exp2(srd - m)], axis=1).astype(BF16))
        win_ref[g] = _dot(jnp.concatenate(probs, axis=0), vw)

    tq_c = t0 + lax.broadcasted_iota(jnp.int32, (C, NC), 0)
    cend = lax.broadcasted_iota(jnp.int32, (C, NC), 1) * CMP_STRIDE + (CMP_LEN - 1)
    vis_bias = jnp.where(cend <= tq_c, 0.0, NEG_INF)
    sees_any = t0 + lax.broadcasted_iota(jnp.int32, (C, 1), 0) >= CMP_LEN - 1
    kc_aug = jnp.concatenate([kcmp_ref[...], caug_ref[...]], axis=1)
    for g in range(GROUPS):
        sc = _dot_nt(win_queries(g), kc_aug)
        psum = jnp.zeros((C, NC), F32)
        probs = []
        for r in range(REP):
            s = sc[r * C:(r + 1) * C] + vis_bias
            e = jnp.exp2(s - jnp.max(s, axis=-1, keepdims=True))
            p = e * jnp.where(sees_any, 1.0 / jnp.sum(e, axis=-1, keepdims=True), 0.0)
            psum = psum + p
            probs.append(p.astype(BF16))
        o_cmp = _dot(jnp.concatenate(probs, axis=0), vcmp_ref[...])
        for r in range(REP):
            hcol = 3 * (g * REP + r)
            part_ref[g, r * C:(r + 1) * C, :] = gates[:, hcol:hcol + 1] * o_cmp[r * C:(r + 1) * C]

        ovt = ovt_ref[...]
        p_hi = psum.astype(BF16)
        r1 = psum - p_hi.astype(F32)
        p_mid = r1.astype(BF16)
        p_lo = (r1 - p_mid.astype(F32)).astype(BF16)
        imp = _dot_nt(ovt, p_hi) + _dot_nt(ovt, p_mid) + _dot_nt(ovt, p_lo)

        jblk = lax.broadcasted_iota(jnp.int32, (NSB, C), 0)
        jt = lax.shift_right_logical(t0 + lax.broadcasted_iota(jnp.int32, (NSB, C), 1), 6)
        forced = (jblk == 0) | (jblk == jt) | (jblk == jt - 1)
        visible = jblk <= jt
        imp = jnp.where(visible, jnp.where(forced, FORCE_BONUS, imp), NEG_INF)
        imp_ref[g] = imp

    rank_ref[...] = jnp.zeros(rank_ref.shape, F32)
    jrow = lax.broadcasted_iota(jnp.int32, (8, LANES), 0)

    def rank_pass(nblk):
        nv = nblk // 8
        for g in range(GROUPS):
            for c0 in range(0, C, LANES):
                cols = slice(c0, c0 + LANES)
                blks = [imp_ref[g, 8 * v:8 * v + 8, cols] for v in range(nv)]
                cnt = [jnp.zeros((8, LANES), F32) for _ in range(nv)]
                for i in range(nblk):
                    row = jnp.broadcast_to(imp_ref[g, i:i + 1, cols], (8, LANES))
                    for v in range(nv):
                        if i < 8 * v:
                            before = row >= blks[v]
                        elif i >= 8 * v + 8:
                            before = row > blks[v]
                        else:
                            before = (row > blks[v]) | ((row == blks[v]) & (jrow > i - 8 * v))
                        cnt[v] = cnt[v] + jnp.where(before, 1.0, 0.0)
                rank_ref[g, 0:nblk, cols] = jnp.concatenate(cnt, axis=0)

    n_seen = (t0 + C) // SEL_LEN
    sizes = [n for n in (NSB // 2, 3 * NSB // 4) if n > n_top and n % 8 == 0] + [NSB]
    lower = n_top
    for n in sizes:
        pl.when((n_seen > lower) & (n_seen <= n))(functools.partial(rank_pass, n))
        lower = n

    for g in range(GROUPS):
        selb_chunks = []
        for c0 in range(0, C, LANES):
            rank = rank_ref[g, :, c0:c0 + LANES]
            visible_c = (lax.broadcasted_iota(jnp.int32, (NSB, LANES), 0)
                         <= lax.shift_right_logical(t0 + c0 + lax.broadcasted_iota(jnp.int32, (NSB, LANES), 1), 6))
            sel_t = jnp.where(visible_c & (rank < n_top), 0.0, -MASK_BIG)
            sel_t = jnp.concatenate([sel_t, jnp.zeros((LANES - NSB, LANES), F32)], axis=0)
            selb_chunks.append(sel_t.T)
        selb = jnp.concatenate(selb_chunks, axis=0)
        for r in range(REP):
            feat = qfeat_ref[g * REP + r:g * REP + r + 1, :]
            qa_ref[g, r * C:(r + 1) * C, 0:LANES] = qg[g][r * C:(r + 1) * C]
            qa_ref[g, r * C:(r + 1) * C, LANES:2 * LANES] = (selb + feat).astype(BF16)

    assert TK == 2 * C
    tq = t0 + lax.broadcasted_iota(jnp.int32, (C, C), 0)
    kcol = lax.broadcasted_iota(jnp.int32, (C, C), 1)

    def load_tile(k0, width):
        kk = jnp.concatenate([ksl_ref[pl.ds(k0, width), :], kaug_ref[pl.ds(k0, width), :]], axis=1)
        vv = jnp.concatenate([vsl_ref[pl.ds(k0, width), :], jnp.ones((width, LANES), BF16)], axis=1)
        return kk, vv

    kk, vv = load_tile(pl.multiple_of(t0, C), C)
    cb = jnp.where(kcol <= lax.broadcasted_iota(jnp.int32, (C, C), 0), 0.0, NEG_INF)
    for g in range(GROUPS):
        s = _dot_nt(qa_ref[g], kk)
        probs = []
        for r in range(REP):
            rows = slice(r * C, (r + 1) * C)
            sr = s[rows] + cb
            m = jnp.max(sr, axis=-1, keepdims=True)
            m_ref[g, rows, :] = jnp.broadcast_to(m, (C, LANES))
            probs.append(jnp.exp2(sr - m).astype(BF16))
        acc_ref[0, g] = _dot(jnp.concatenate(probs, axis=0), vv)
    acc_ref[1] = jnp.zeros(acc_ref.shape[1:], F32)

    def past_tile(k0, width):
        kk, vv = load_tile(k0, width)
        outs = []
        for g in range(GROUPS):
            s = _dot_nt(qa_ref[g], kk)
            probs = [jnp.exp2(s[r * C:(r + 1) * C] - _rep_lanes(m_ref[g, r * C:(r + 1) * C, :], width // LANES))
                     .astype(BF16) for r in range(REP)]
            outs.append(_dot(jnp.concatenate(probs, axis=0), vv))
        return outs

    n_full = t0 // TK

    def pair_body(k, carry):
        a = past_tile(pl.multiple_of(2 * k * TK, TK), TK)
        b = past_tile(pl.multiple_of((2 * k + 1) * TK, TK), TK)
        for g in range(GROUPS):
            acc_ref[0, g] += a[g]
            acc_ref[1, g] += b[g]
        return carry

    lax.fori_loop(0, n_full // 2, pair_body, 0)

    @pl.when(n_full % 2 == 1)
    def _():
        a = past_tile(pl.multiple_of((n_full - 1) * TK, TK), TK)
        for g in range(GROUPS):
            acc_ref[0, g] += a[g]

    @pl.when(t0 % TK == C)
    def _():
        a = past_tile(pl.multiple_of(t0 - C, C), C)
        for g in range(GROUPS):
            acc_ref[1, g] += a[g]

    def tree_sum(parts):
        while len(parts) > 1:
            parts = [parts[i] + parts[i + 1] for i in range(0, len(parts), 2)]
        return parts[0]

    def combine():
        lane = lax.broadcasted_iota(jnp.int32, (C, LANES), 1)
        seen = []
        for r in range(REP):
            chunk = None
            for g in range(GROUPS):
                hcol = 3 * (g * REP + r)
                rows = slice(r * C, (r + 1) * C)
                sel = acc_ref[0, g, rows, :] + acc_ref[1, g, rows, :]
                win = win_ref[g, rows, :]
                seen += [sel, win]
                o = (part_ref[g, rows, :]
                     + gates[:, hcol + 1:hcol + 2] * (sel[:, :LANES] * (1.0 / sel[:, LANES:]))
                     + gates[:, hcol + 2:hcol + 3] * (win[:, :LANES] * (1.0 / win[:, LANES:])))
                chunk = o if g == 0 else jnp.where(lane < HEAD_DIM, chunk, o)
            out_ref[:, r * LANES:(r + 1) * LANES] = chunk.astype(BF16)
        return jnp.sum(tree_sum(seen))

    @pl.when(jnp.logical_not(jnp.isfinite(combine())))
    def _():
        kd, kp, vw = win_operands()
        for g in range(GROUPS):
            qw = win_queries(g)
            sd = _dot_nt(qw, kd)
            sp = _dot_nt(qw, kp)
            probs = []
            for r in range(REP):
                rows = slice(r * C, (r + 1) * C)
                sr = jnp.concatenate([sp[rows] + past_bias, sd[rows] + causal_bias], axis=1)
                probs.append(jnp.exp2(sr - jnp.max(sr, axis=-1, keepdims=True)).astype(BF16))
            win_ref[g] = _dot(jnp.concatenate(probs, axis=0), vw)

        m_ref[...] = jnp.full(m_ref.shape, M_INIT, F32)
        acc_ref[...] = jnp.zeros(acc_ref.shape, F32)

        def online_tile(kt, carry):
            k0 = pl.multiple_of(kt * C, C)
            kk, vv = load_tile(k0, C)
            cb = jnp.where(k0 + kcol <= tq, 0.0, NEG_INF)
            for g in range(GROUPS):
                s = _dot_nt(qa_ref[g], kk)
                probs = []
                for r in range(REP):
                    rows = slice(r * C, (r + 1) * C)
                    sr = s[rows] + cb
                    m_old = m_ref[g, rows, :]
                    m_new = jnp.maximum(m_old, jnp.max(sr, axis=-1, keepdims=True))
                    acc_ref[0, g, rows, :] = _rep_lanes(jnp.exp2(m_old - m_new), 2) * acc_ref[0, g, rows, :]
                    m_ref[g, rows, :] = m_new
                    probs.append(jnp.exp2(sr - _rep_lanes(m_new, C // LANES)).astype(BF16))
                acc_ref[0, g] += _dot(jnp.concatenate(probs, axis=0), vv)
            return carry

        lax.fori_loop(0, t0 // C + 1, online_tile, 0)
        combine()


def _attention(qn, gates, kcmp, vcmp, nkv, kaug, caug, qfeat, ovt, sinks_l2, swa_qkv, swa_bias, *, C, TK):
    B, S, _ = qn.shape
    NC = kcmp.shape[1]
    NSB = S // SEL_LEN
    n_top = min(N_SEL, NSB)
    W = NSA_WINDOW
    body = functools.partial(_nsa_body, C=C, TK=TK, W=W, NSB=NSB, NC=NC, n_top=n_top)
    qblk = lambda n: pl.BlockSpec((None, C, n), lambda b, i: (b, i, 0))
    per_b = lambda n: pl.BlockSpec((None, n, 128), lambda b, i: (b, 0, 0))
    nkv_col = lambda j: pl.BlockSpec((None, S, 128), lambda b, i, j=j: (b, 0, j))
    full = lambda a: pl.BlockSpec(a.shape, lambda b, i: (0,) * a.ndim)
    sub = C // SWA_WINDOW
    swa_cur = lambda j: pl.BlockSpec((None, C, 128), lambda b, i, j=j: (b, i, j))
    swa_prev = lambda j: pl.BlockSpec((None, SWA_WINDOW, 128), lambda b, i, j=j: (b, jnp.maximum(i * sub - 1, 0), j))
    smem = pl.BlockSpec(memory_space=pltpu.SMEM)
    return pl.pallas_call(
        body,
        grid=(B, S // C),
        in_specs=[qblk(512), qblk(128), per_b(NC), per_b(NC),
                  nkv_col(0), nkv_col(1), nkv_col(2), nkv_col(3),
                  full(kaug), full(caug), full(qfeat), full(ovt),
                  smem, qblk(512), swa_prev(4), swa_cur(4), swa_prev(5), swa_cur(5), full(swa_bias)],
        out_specs=[qblk(512), qblk(512)],
        out_shape=[jax.ShapeDtypeStruct((B, S, 512), BF16)] * 2,
        scratch_shapes=[
            pltpu.VMEM((GROUPS, REP * C, 2 * LANES), BF16),
            pltpu.VMEM((GROUPS, REP * C, LANES), F32),
            pltpu.VMEM((2, GROUPS, REP * C, 2 * LANES), F32),
            pltpu.VMEM((GROUPS, REP * C, 2 * LANES), F32),
            pltpu.VMEM((GROUPS, REP * C, LANES), F32),
            pltpu.VMEM((GROUPS, NSB, C), F32),
            pltpu.VMEM((GROUPS, NSB, C), F32),
        ],
        compiler_params=_params(("parallel", "parallel")),
        name="attention",
    )(qn, gates, kcmp, vcmp, nkv, nkv, nkv, nkv, kaug, caug, qfeat, ovt,
      sinks_l2, swa_qkv, swa_qkv, swa_qkv, swa_qkv, swa_qkv, swa_bias)


def _swa_body(sinks_ref, q_ref, kp_ref, kc_ref, vp_ref, vc_ref, bias_ref, out_ref, *, C):
    WB = SWA_WINDOW
    at_start = jnp.where(pl.program_id(1) == 0, 1, 0)
    kall = jnp.concatenate([kp_ref[...], kc_ref[...]], axis=0)
    vall = jnp.concatenate([vp_ref[...], vc_ref[...]], axis=0)
    vall = jnp.concatenate([vall, jnp.ones((WB + C, LANES), BF16)], axis=1)
    lane = lax.broadcasted_iota(jnp.int32, (WB, LANES), 1)
    for sb in range(C // WB):
        q_sb = q_ref[sb * WB:(sb + 1) * WB, :]
        kk = kall[sb * WB:(sb + 2) * WB]
        vv = vall[sb * WB:(sb + 2) * WB]
        table = at_start if sb == 0 else 0
        outs = []
        for g in range(GROUPS):
            s_all = _dot_nt(_group_rows(q_sb, g, WB), kk)
            probs, sink_terms = [], []
            for r in range(REP):
                h = g * REP + r
                s = s_all[r * WB:(r + 1) * WB] + bias_ref[table, h]
                sink = sinks_ref[h]
                m = jnp.maximum(jnp.max(s, axis=-1, keepdims=True), sink)
                probs.append(jnp.exp2(s - m).astype(BF16))
                sink_terms.append(jnp.exp2(sink - m))
            o = _dot(jnp.concatenate(probs, axis=0), vv)
            outs.append([o[r * WB:(r + 1) * WB, :LANES] * (1.0 / (o[r * WB:(r + 1) * WB, LANES:] + sink_terms[r]))
                         for r in range(REP)])
        for r in range(REP):
            chunk = jnp.where(lane < HEAD_DIM, outs[0][r], outs[1][r])
            out_ref[sb * WB:(sb + 1) * WB, r * LANES:(r + 1) * LANES] = chunk.astype(BF16)


def _swa_bias(slopes_l2, C):
    qi = np.arange(C)[:, None]
    kj = np.arange(2 * C)[None, :]
    dist = C + qi - kj
    valid = (dist >= 0) & (dist < SWA_WINDOW)
    tab = -slopes_l2[:, None, None] * jnp.asarray(dist, F32)[None]
    return jnp.stack([jnp.where(valid[None], tab, NEG_INF), jnp.where((valid & (kj >= C))[None], tab, NEG_INF)])


CONV_TAIL = 8
FFN_CHUNK = 256


def _ffn_body(on_ref, os_ref, x_ref, p_ref, wn_ref, ws_ref, g0_ref, g1_ref, wg_ref, wu_ref, cw_ref, cb_ref,
              wd_ref, g2_ref, wple_ref, wpg_ref, out_ref, act_ref, tail_ref, *, tm, tiles_per_seq):
    @pl.when(pl.program_id(0) % tiles_per_seq == 0)
    def _():
        tail_ref[...] = jnp.zeros(tail_ref.shape, F32)

    mix = _dot(on_ref[...], wn_ref[...]) + _dot(os_ref[...], ws_ref[...])
    x1 = x_ref[...] + _rms(mix, g0_ref[...])
    h = _rms(x1, g1_ref[...]).astype(BF16)
    for c in range(D_FF // FFN_CHUNK):
        cols = slice(c * FFN_CHUNK, (c + 1) * FFN_CHUNK)
        a = _dot(h, wg_ref[:, cols])
        u = _dot(h, wu_ref[:, cols])
        ae = jnp.concatenate([tail_ref[:, cols], a], axis=0)
        tail_ref[:, cols] = a[tm - CONV_TAIL:]
        conv = (cw_ref[2:3, cols] * a
                + cw_ref[1:2, cols] * pltpu.roll(ae, 1, 0)[CONV_TAIL:]
                + cw_ref[0:1, cols] * pltpu.roll(ae, 2, 0)[CONV_TAIL:]
                + cb_ref[:, cols])
        act_ref[:, cols] = (_gelu_tanh(conv) * u).astype(BF16)
    y = _dot(act_ref[...], wd_ref[...])
    x2 = x1 + _rms(y, g2_ref[...])
    e = _dot(p_ref[...].astype(BF16), wple_ref[...])
    gate = jax.nn.sigmoid(_dot(x2.astype(BF16), wpg_ref[...]))
    out_ref[...] = x2 + e * gate


def _ffn(o_nsa, o_swa, x2d, p2d, wn, ws, g0, g1, wg, wu, cw, cb, wd, g2, wple, wpg, tm, S):
    T = x2d.shape[0]
    body = functools.partial(_ffn_body, tm=tm, tiles_per_seq=S // tm)
    row = lambda n: pl.BlockSpec((tm, n), lambda i: (i, 0))
    const = lambda a: pl.BlockSpec(a.shape, lambda i: (0,) * a.ndim, pipeline_mode=pl.Buffered(1))
    weights = (wn, ws, g0, g1, wg, wu, cw, cb, wd, g2, wple, wpg)
    return pl.pallas_call(
        body,
        grid=(T // tm,),
        in_specs=[row(512), row(512), row(D_MODEL), row(PLE_DIM)] + [const(a) for a in weights],
        out_specs=row(D_MODEL),
        out_shape=jax.ShapeDtypeStruct((T, D_MODEL), F32),
        scratch_shapes=[pltpu.VMEM((tm, D_FF), BF16), pltpu.VMEM((CONV_TAIL, D_FF), F32)],
        compiler_params=_params(("arbitrary",)),
        name="ffn",
    )(o_nsa, o_swa, x2d, p2d, *weights)


def _alibi_slopes():
    h = jnp.arange(N_HEADS, dtype=F32)
    return jnp.exp2(-8.0 * (h + 1.0) / N_HEADS)


def _key_features(S):
    pos = np.arange(S)
    blk, off = pos // SEL_LEN, pos % SEL_LEN
    feat = np.zeros((S, LANES), np.float32)
    feat[pos, blk] = 1.0
    base = S // SEL_LEN
    assert base <= HEAD_DIM
    for k in range(N_ALIBI_TERMS):
        feat[:, HEAD_DIM + k] = blk * SEL_LEN
        feat[:, HEAD_DIM + N_ALIBI_TERMS + k] = off
    return jnp.asarray(feat, BF16)


def _cmp_features(NC):
    assert NC <= 256
    starts = np.arange(NC) * CMP_STRIDE
    feat = np.zeros((NC, LANES), np.float32)
    for k in range(N_ALIBI_TERMS):
        feat[:, HEAD_DIM + k] = starts
        feat[:, HEAD_DIM + N_ALIBI_TERMS + k] = CMP_LEN - 1
    return jnp.asarray(feat, BF16)


def _query_features(slopes):
    terms, rest = [], slopes
    for _ in range(N_ALIBI_TERMS):
        t = rest.astype(BF16).astype(F32)
        terms.append(t)
        rest = rest - t
    tt = jnp.stack(terms, axis=1)
    feat = jnp.zeros((slopes.shape[0], LANES), F32)
    feat = feat.at[:, HEAD_DIM:HEAD_DIM + N_ALIBI_TERMS].set(tt)
    feat = feat.at[:, HEAD_DIM + N_ALIBI_TERMS:HEAD_DIM + 2 * N_ALIBI_TERMS].set(tt)
    return feat


def _overlap_t(S, NC):
    n = np.arange(NC)[None, :]
    j = np.arange(S // SEL_LEN)[:, None]
    cs, ss = n * CMP_STRIDE, j * SEL_LEN
    ov = (cs < ss + SEL_LEN) & (cs + CMP_LEN - 1 >= ss)
    return jnp.asarray(ov.astype(np.float32), BF16)


def _head_cols(perm):
    return np.concatenate([np.arange(h * HEAD_DIM, (h + 1) * HEAD_DIM) for h in perm])


def kernel(x, p, attn_pre_g, w_in, cmp_pe_k, cmp_w1_k, cmp_w2_k, cmp_pe_v, cmp_w1_v, cmp_w2_v, sinks, w_o,
           attn_post_g, mlp_pre_g, w_gate_up, conv_w, conv_b, w_down, mlp_post_g, w_ple, w_ple_gate):
    B, S, D = x.shape
    T = B * S
    depth = w_in.shape[0]
    slopes = _alibi_slopes()
    swa_slopes, nsa_slopes = slopes[:SWA_HEADS] * LOG2E, slopes[SWA_HEADS:] * LOG2E
    NS = S // CMP_STRIDE
    kaug = _key_features(S)
    caug = _cmp_features(NS)
    qfeat = _query_features(nsa_slopes)
    ovt = _overlap_t(S, NS)
    swa_bias = _swa_bias(swa_slopes, SWA_WINDOW)
    perm = _head_cols(HEAD_PERM)
    scale = HEAD_DIM ** -0.5 * LOG2E
    tm = TOKEN_TILE

    xf = x.reshape(T, D)
    for i in range(depth):
        w = w_in[i]
        q_n = w[:, 0:512][:, perm] * scale
        q_s = w[:, 1304:1816][:, perm] * scale
        gate_cols = jnp.pad(w[:, 1280:1304], ((0, 0), (0, LANES - 24)))
        w_cat = jnp.concatenate([q_n, w[:, 512:1280], q_s, w[:, 1816:2072], gate_cols], axis=1).astype(BF16)
        qn, kc, vc, nkv, swa_qkv, gates = _inproj(xf, attn_pre_g[i][None, :], w_cat, tm)

        wk = _compress_weights(cmp_pe_k[i], cmp_w1_k[i], cmp_w2_k[i])
        wv = _compress_weights(cmp_pe_v[i], cmp_w1_v[i], cmp_w2_v[i])
        seg_w = CMP_STRIDE * GROUPS * HEAD_DIM
        kcmp, vcmp = _compress(kc.reshape(B, NS, seg_w), vc.reshape(B, NS, seg_w), wk, wv)

        o_nsa, o_swa = _attention(qn.reshape(B, S, 512), gates.reshape(B, S, 128), kcmp, vcmp,
                                  nkv.reshape(B, S, 512), kaug, caug, qfeat, ovt, sinks[i] * LOG2E,
                                  swa_qkv.reshape(B, S, 768), swa_bias, C=ATTN_TILE, TK=2 * ATTN_TILE)

        wo = w_o[i]
        wgu = w_gate_up[i].astype(BF16)
        xf = _ffn(o_nsa.reshape(T, 512), o_swa.reshape(T, 512), xf, p[i].reshape(T, PLE_DIM),
                  wo[:512][perm].astype(BF16), wo[512:][perm].astype(BF16), attn_post_g[i][None, :],
                  mlp_pre_g[i][None, :], wgu[:, :D_FF], wgu[:, D_FF:], conv_w[i], conv_b[i][None, :],
                  w_down[i].astype(BF16), mlp_post_g[i][None, :], w_ple[i].astype(BF16),
                  w_ple_gate[i].astype(BF16), tm, S)
    return xf.reshape(B, S, D)
```

```python
import functools

import numpy as np
import jax
import jax.numpy as jnp
from jax import lax
from jax.experimental import pallas as pl
from jax.experimental.pallas import tpu as pltpu

F32 = jnp.float32
BF16 = jnp.bfloat16

D_MODEL = 1024
HEAD_DIM = 64
NSA_HEADS = 8
SWA_HEADS = 8
N_HEADS = NSA_HEADS + SWA_HEADS
CMP_LEN = 32
CMP_STRIDE = 16
CMP_HIDDEN = 128
SEL_LEN = 64
N_SEL = 16
NSA_WINDOW = 512
SWA_WINDOW = 128
D_FF = 2816
CONV_W = 3
PLE_DIM = 256
RMS_EPS = 1e-6
NEG_INF = -1e30
FORCE_BONUS = 1e6

LANES = 128
MASK_BIG = 2.0 ** 100
M_INIT = -3.0e38
GROUPS = 2
REP = 4
Q_W = GROUPS * REP * HEAD_DIM
KV_W = GROUPS * HEAD_DIM
GATE_W = 3 * NSA_HEADS
SLAB_WIDTHS = (Q_W, KV_W, KV_W, 4 * KV_W, Q_W + 2 * KV_W, LANES)
HEAD_PERM = (0, 4, 1, 5, 2, 6, 3, 7)
N_ALIBI_TERMS = 3
LOG2E = 1.4426950408889634

VMEM_LIMIT = 56 * 1024 * 1024
TOKEN_TILE = 1024
ATTN_TILE = 256


def _rms(x, g):
    return x * lax.rsqrt(jnp.mean(x * x, axis=-1, keepdims=True) + RMS_EPS) * g


def _gelu_tanh(x):
    c = np.sqrt(2.0 / np.pi).astype(np.float32)
    return x * (0.5 * (1.0 + jnp.tanh(c * (x + 0.044715 * (x * x * x)))))


def _dot(a, b):
    return jnp.dot(a, b, preferred_element_type=F32)


def _dot_nt(a, b):
    return lax.dot_general(a, b, (((1,), (1,)), ((), ())), preferred_element_type=F32)


def _rep_lanes(a, n):
    return a if n == 1 else jnp.concatenate([a] * n, axis=1)


def _params(sem):
    return pltpu.CompilerParams(dimension_semantics=sem, vmem_limit_bytes=VMEM_LIMIT)


def _inproj_body(x_ref, g_ref, w_ref, qn_ref, kc_ref, vc_ref, nkv_ref, swa_ref, gates_ref):
    h = _rms(x_ref[...], g_ref[...]).astype(BF16)
    proj = _dot(h, w_ref[...])
    off = 0
    for ref, width in zip((qn_ref, kc_ref, vc_ref, nkv_ref, swa_ref), SLAB_WIDTHS):
        ref[...] = proj[:, off:off + width].astype(BF16)
        off += width
    gates_ref[...] = jax.nn.sigmoid(proj[:, off:off + LANES])


def _inproj(x2d, g, w_cat, tm):
    T = x2d.shape[0]
    wn = w_cat.shape[1]
    row = lambda n: pl.BlockSpec((tm, n), lambda i: (i, 0))
    full = lambda a: pl.BlockSpec(a.shape, lambda i: (0,) * a.ndim)
    return pl.pallas_call(
        _inproj_body,
        grid=(T // tm,),
        in_specs=[row(D_MODEL), full(g), full(w_cat)],
        out_specs=[row(width) for width in SLAB_WIDTHS],
        out_shape=[jax.ShapeDtypeStruct((T, width), BF16) for width in SLAB_WIDTHS[:-1]]
        + [jax.ShapeDtypeStruct((T, LANES), F32)],
        compiler_params=_params(("parallel",)),
        name="inproj",
    )(x2d, g, w_cat)


def _compress_body(kc_ref, vc_ref,
                   pea_k, peb_k, w1a_k, w1b_k, w2_k,
                   pea_v, peb_v, w1a_v, w1b_v, w2_v,
                   kcmp_ref, vcmp_ref):
    def one(seg_ref, pea, peb, w1a, w1b, w2, out_ref):
        seg = seg_ref[...].astype(F32)
        ns = seg.shape[0]
        ha = _dot((seg + pea[...]).astype(BF16), w1a[...])
        hb = _dot((seg + peb[...]).astype(BF16), w1b[...])
        hid = ha + pltpu.roll(hb, ns - 1, 0)
        act = _gelu_tanh(hid).astype(BF16)
        out_ref[...] = _dot(act, w2[...]).astype(BF16)

    one(kc_ref, pea_k, peb_k, w1a_k, w1b_k, w2_k, kcmp_ref)
    one(vc_ref, pea_v, peb_v, w1a_v, w1b_v, w2_v, vcmp_ref)


def _compress_weights(pe, w1, w2):
    half = CMP_LEN // 2
    eye = jnp.eye(GROUPS, dtype=F32)
    w1r = w1.reshape(CMP_LEN, HEAD_DIM, CMP_HIDDEN)

    def expand(w):
        return jnp.einsum("ldj,pg->lpdgj", w, eye).reshape(half * GROUPS * HEAD_DIM, GROUPS * CMP_HIDDEN)

    def pe_row(p):
        return jnp.broadcast_to(p[:, None, :], (half, GROUPS, HEAD_DIM)).reshape(1, half * GROUPS * HEAD_DIM)

    w2e = jnp.einsum("jd,pg->pjgd", w2, eye).reshape(GROUPS * CMP_HIDDEN, GROUPS * HEAD_DIM)
    return (pe_row(pe[:half]), pe_row(pe[half:]),
            expand(w1r[:half]).astype(BF16), expand(w1r[half:]).astype(BF16), w2e.astype(BF16))


def _compress(kc_seg, vc_seg, wk, wv):
    B, NS, WD = kc_seg.shape
    seg = pl.BlockSpec((None, NS, WD), lambda b: (b, 0, 0))
    full = lambda a: pl.BlockSpec(a.shape, lambda b: (0,) * a.ndim)
    out = pl.BlockSpec((None, NS, KV_W), lambda b: (b, 0, 0))
    return pl.pallas_call(
        _compress_body,
        grid=(B,),
        in_specs=[seg, seg] + [full(a) for a in wk] + [full(a) for a in wv],
        out_specs=[out, out],
        out_shape=[jax.ShapeDtypeStruct((B, NS, KV_W), BF16)] * 2,
        compiler_params=_params(("parallel",)),
        name="compress",
    )(kc_seg, vc_seg, *wk, *wv)


def _group_rows(q_all, g, C):
    lane = lax.broadcasted_iota(jnp.int32, (C, LANES), 1)
    keep = (lane < HEAD_DIM) if g == 0 else (lane >= HEAD_DIM)
    zero = jnp.zeros((C, LANES), q_all.dtype)
    return jnp.concatenate(
        [jnp.where(keep, q_all[:, r * LANES:(r + 1) * LANES], zero) for r in range(REP)], axis=0)


def _nsa_body(qn_ref, gates_ref, kcmp_ref, vcmp_ref, ksl_ref, vsl_ref, kwn_ref, vwn_ref,
              kaug_ref, caug_ref, qfeat_ref, ovt_ref,
              sinks_ref, sq_ref, skp_ref, skc_ref, svp_ref, svc_ref, sbias_ref,
              out_ref, swa_out_ref,
              qa_ref, m_ref, acc_ref, win_ref, part_ref, imp_ref, rank_ref, *, C, TK, W, NSB, NC, n_top):
    t0 = pl.program_id(1) * C
    _swa_body(sinks_ref, sq_ref, skp_ref, skc_ref, svp_ref, svc_ref, sbias_ref, swa_out_ref, C=C)
    q_all = qn_ref[...]
    qg = [_group_rows(q_all, g, C) for g in range(GROUPS)]
    gates = gates_ref[...]

    assert C <= W
    rowi = lax.broadcasted_iota(jnp.int32, (C, C), 0)
    coli = lax.broadcasted_iota(jnp.int32, (C, C), 1)
    causal_bias = jnp.where(coli <= rowi, 0.0, NEG_INF)
    pstart = pl.multiple_of(jnp.maximum(t0 - W, 0), C)
    ppos = pstart + lax.broadcasted_iota(jnp.int32, (C, W), 1)
    pdist = (t0 + lax.broadcasted_iota(jnp.int32, (C, W), 0)) - ppos
    past_bias = jnp.where((pdist < W) & (ppos < t0), 0.0, NEG_INF)

    def win_operands():
        td = pl.multiple_of(t0, C)
        kd = jnp.concatenate([kwn_ref[pl.ds(td, C), :], kaug_ref[pl.ds(td, C), :]], axis=1)
        kp = jnp.concatenate([kwn_ref[pl.ds(pstart, W), :], kaug_ref[pl.ds(pstart, W), :]], axis=1)
        vals = jnp.concatenate([vwn_ref[pl.ds(pstart, W), :], vwn_ref[pl.ds(td, C), :]], axis=0)
        return kd, kp, jnp.concatenate([vals, jnp.ones((W + C, LANES), BF16)], axis=1)

    def win_queries(g):
        feats = jnp.concatenate(
            [jnp.broadcast_to(qfeat_ref[g * REP + r:g * REP + r + 1, :], (C, LANES)) for r in range(REP)], axis=0)
        return jnp.concatenate([qg[g], feats.astype(BF16)], axis=1)

    kd, kp, vw = win_operands()
    for g in range(GROUPS):
        qw = win_queries(g)
        sd = _dot_nt(qw, kd)
        sp = _dot_nt(qw, kp)
        probs = []
        for r in range(REP):
            rows = slice(r * C, (r + 1) * C)
            srd = sd[rows] + causal_bias
            m = jnp.max(srd, axis=-1, keepdims=True)
            probs.append(jnp.concatenate([jnp.exp2(sp[rows] + past_bias - m), jnp.exp2(srd - m)], axis=1).astype(BF16))
        win_ref[g] = _dot(jnp.concatenate(probs, axis=0), vw)

    tq_c = t0 + lax.broadcasted_iota(jnp.int32, (C, NC), 0)
    cend = lax.broadcasted_iota(jnp.int32, (C, NC), 1) * CMP_STRIDE + (CMP_LEN - 1)
    vis_bias = jnp.where(cend <= tq_c, 0.0, NEG_INF)
    sees_any = t0 + lax.broadcasted_iota(jnp.int32, (C, 1), 0) >= CMP_LEN - 1
    kc_aug = jnp.concatenate([kcmp_ref[...], caug_ref[...]], axis=1)
    for g in range(GROUPS):
        sc = _dot_nt(win_queries(g), kc_aug)
        psum = jnp.zeros((C, NC), F32)
        probs = []
        for r in range(REP):
            s = sc[r * C:(r + 1) * C] + vis_bias
            e = jnp.exp2(s - jnp.max(s, axis=-1, keepdims=True))
            p = e * jnp.where(sees_any, 1.0 / jnp.sum(e, axis=-1, keepdims=True), 0.0)
            psum = psum + p
            probs.append(p.astype(BF16))
        o_cmp = _dot(jnp.concatenate(probs, axis=0), vcmp_ref[...])
        for r in range(REP):
            hcol = 3 * (g * REP + r)
            part_ref[g, r * C:(r + 1) * C, :] = gates[:, hcol:hcol + 1] * o_cmp[r * C:(r + 1) * C]

        ovt = ovt_ref[...]
        p_hi = psum.astype(BF16)
        r1 = psum - p_hi.astype(F32)
        p_mid = r1.astype(BF16)
        p_lo = (r1 - p_mid.astype(F32)).astype(BF16)
        imp = _dot_nt(ovt, p_hi) + _dot_nt(ovt, p_mid) + _dot_nt(ovt, p_lo)

        jblk = lax.broadcasted_iota(jnp.int32, (NSB, C), 0)
        jt = lax.shift_right_logical(t0 + lax.broadcasted_iota(jnp.int32, (NSB, C), 1), 6)
        forced = (jblk == 0) | (jblk == jt) | (jblk == jt - 1)
        visible = jblk <= jt
        imp = jnp.where(visible, jnp.where(forced, FORCE_BONUS, imp), NEG_INF)
        imp_ref[g] = imp

    rank_ref[...] = jnp.zeros(rank_ref.shape, F32)
    jrow = lax.broadcasted_iota(jnp.int32, (8, LANES), 0)

    def rank_pass(nblk):
        nv = nblk // 8
        for g in range(GROUPS):
            for c0 in range(0, C, LANES):
                cols = slice(c0, c0 + LANES)
                blks = [imp_ref[g, 8 * v:8 * v + 8, cols] for v in range(nv)]
                cnt = [jnp.zeros((8, LANES), F32) for _ in range(nv)]
                for i in range(nblk):
                    row = jnp.broadcast_to(imp_ref[g, i:i + 1, cols], (8, LANES))
                    for v in range(nv):
                        if i < 8 * v:
                            before = row >= blks[v]
                        elif i >= 8 * v + 8:
                            before = row > blks[v]
                        else:
                            before = (row > blks[v]) | ((row == blks[v]) & (jrow > i - 8 * v))
                        cnt[v] = cnt[v] + jnp.where(before, 1.0, 0.0)
                rank_ref[g, 0:nblk, cols] = jnp.concatenate(cnt, axis=0)

    n_seen = (t0 + C) // SEL_LEN
    sizes = [n for n in (NSB // 2, 3 * NSB // 4) if n > n_top and n % 8 == 0] + [NSB]
    lower = n_top
    for n in sizes:
        pl.when((n_seen > lower) & (n_seen <= n))(functools.partial(rank_pass, n))
        lower = n

    for g in range(GROUPS):
        selb_chunks = []
        for c0 in range(0, C, LANES):
            rank = rank_ref[g, :, c0:c0 + LANES]
            visible_c = (lax.broadcasted_iota(jnp.int32, (NSB, LANES), 0)
                         <= lax.shift_right_logical(t0 + c0 + lax.broadcasted_iota(jnp.int32, (NSB, LANES), 1), 6))
            sel_t = jnp.where(visible_c & (rank < n_top), 0.0, -MASK_BIG)
            sel_t = jnp.concatenate([sel_t, jnp.zeros((LANES - NSB, LANES), F32)], axis=0)
            selb_chunks.append(sel_t.T)
        selb = jnp.concatenate(selb_chunks, axis=0)
        for r in range(REP):
            feat = qfeat_ref[g * REP + r:g * REP + r + 1, :]
            qa_ref[g, r * C:(r + 1) * C, 0:LANES] = qg[g][r * C:(r + 1) * C]
            qa_ref[g, r * C:(r + 1) * C, LANES:2 * LANES] = (selb + feat).astype(BF16)

    assert TK == 2 * C
    tq = t0 + lax.broadcasted_iota(jnp.int32, (C, C), 0)
    kcol = lax.broadcasted_iota(jnp.int32, (C, C), 1)

    def load_tile(k0, width):
        kk = jnp.concatenate([ksl_ref[pl.ds(k0, width), :], kaug_ref[pl.ds(k0, width), :]], axis=1)
        vv = jnp.concatenate([vsl_ref[pl.ds(k0, width), :], jnp.ones((width, LANES), BF16)], axis=1)
        return kk, vv

    kk, vv = load_tile(pl.multiple_of(t0, C), C)
    cb = jnp.where(kcol <= lax.broadcasted_iota(jnp.int32, (C, C), 0), 0.0, NEG_INF)
    for g in range(GROUPS):
        s = _dot_nt(qa_ref[g], kk)
        probs = []
        for r in range(REP):
            rows = slice(r * C, (r + 1) * C)
            sr = s[rows] + cb
            m = jnp.max(sr, axis=-1, keepdims=True)
            m_ref[g, rows, :] = jnp.broadcast_to(m, (C, LANES))
            probs.append(jnp.exp2(sr - m).astype(BF16))
        acc_ref[0, g] = _dot(jnp.concatenate(probs, axis=0), vv)
    acc_ref[1] = jnp.zeros(acc_ref.shape[1:], F32)

    def past_tile(k0, width):
        kk, vv = load_tile(k0, width)
        outs = []
        for g in range(GROUPS):
            s = _dot_nt(qa_ref[g], kk)
            probs = [jnp.exp2(s[r * C:(r + 1) * C] - _rep_lanes(m_ref[g, r * C:(r + 1) * C, :], width // LANES))
                     .astype(BF16) for r in range(REP)]
            outs.append(_dot(jnp.concatenate(probs, axis=0), vv))
        return outs

    n_full = t0 // TK

    def pair_body(k, carry):
        a = past_tile(pl.multiple_of(2 * k * TK, TK), TK)
        b = past_tile(pl.multiple_of((2 * k + 1) * TK, TK), TK)
        for g in range(GROUPS):
            acc_ref[0, g] += a[g]
            acc_ref[1, g] += b[g]
        return carry

    lax.fori_loop(0, n_full // 2, pair_body, 0)

    @pl.when(n_full % 2 == 1)
    def _():
        a = past_tile(pl.multiple_of((n_full - 1) * TK, TK), TK)
        for g in range(GROUPS):
            acc_ref[0, g] += a[g]

    @pl.when(t0 % TK == C)
    def _():
        a = past_tile(pl.multiple_of(t0 - C, C), C)
        for g in range(GROUPS):
            acc_ref[1, g] += a[g]

    def tree_sum(parts):
        while len(parts) > 1:
            parts = [parts[i] + parts[i + 1] for i in range(0, len(parts), 2)]
        return parts[0]

    def combine():
        lane = lax.broadcasted_iota(jnp.int32, (C, LANES), 1)
        seen = []
        for r in range(REP):
            chunk = None
            for g in range(GROUPS):
                hcol = 3 * (g * REP + r)
                rows = slice(r * C, (r + 1) * C)
                sel = acc_ref[0, g, rows, :] + acc_ref[1, g, rows, :]
                win = win_ref[g, rows, :]
                seen += [sel, win]
                o = (part_ref[g, rows, :]
                     + gates[:, hcol + 1:hcol + 2] * (sel[:, :LANES] * (1.0 / sel[:, LANES:]))
                     + gates[:, hcol + 2:hcol + 3] * (win[:, :LANES] * (1.0 / win[:, LANES:])))
                chunk = o if g == 0 else jnp.where(lane < HEAD_DIM, chunk, o)
            out_ref[:, r * LANES:(r + 1) * LANES] = chunk.astype(BF16)
        return jnp.sum(tree_sum(seen))

    @pl.when(jnp.logical_not(jnp.isfinite(combine())))
    def _():
        kd, kp, vw = win_operands()
        for g in range(GROUPS):
            qw = win_queries(g)
            sd = _dot_nt(qw, kd)
            sp = _dot_nt(qw, kp)
            probs = []
            for r in range(REP):
                rows = slice(r * C, (r + 1) * C)
                sr = jnp.concatenate([sp[rows] + past_bias, sd[rows] + causal_bias], axis=1)
                probs.append(jnp.exp2(sr - jnp.max(sr, axis=-1, keepdims=True)).astype(BF16))
            win_ref[g] = _dot(jnp.concatenate(probs, axis=0), vw)

        m_ref[...] = jnp.full(m_ref.shape, M_INIT, F32)
        acc_ref[...] = jnp.zeros(acc_ref.shape, F32)

        def online_tile(kt, carry):
            k0 = pl.multiple_of(kt * C, C)
            kk, vv = load_tile(k0, C)
            cb = jnp.where(k0 + kcol <= tq, 0.0, NEG_INF)
            for g in range(GROUPS):
                s = _dot_nt(qa_ref[g], kk)
                probs = []
                for r in range(REP):
                    rows = slice(r * C, (r + 1) * C)
                    sr = s[rows] + cb
                    m_old = m_ref[g, rows, :]
                    m_new = jnp.maximum(m_old, jnp.max(sr, axis=-1, keepdims=True))
                    acc_ref[0, g, rows, :] = _rep_lanes(jnp.exp2(m_old - m_new), 2) * acc_ref[0, g, rows, :]
                    m_ref[g, rows, :] = m_new
                    probs.append(jnp.exp2(sr - _rep_lanes(m_new, C // LANES)).astype(BF16))
                acc_ref[0, g] += _dot(jnp.concatenate(probs, axis=0), vv)
            return carry

        lax.fori_loop(0, t0 // C + 1, online_tile, 0)
        combine()


def _attention(qn, gates, kcmp, vcmp, nkv, kaug, caug, qfeat, ovt, sinks_l2, swa_qkv, swa_bias, *, C, TK):
    B, S, _ = qn.shape
    NC = kcmp.shape[1]
    NSB = S // SEL_LEN
    n_top = min(N_SEL, NSB)
    W = NSA_WINDOW
    body = functools.partial(_nsa_body, C=C, TK=TK, W=W, NSB=NSB, NC=NC, n_top=n_top)
    qblk = lambda n: pl.BlockSpec((None, C, n), lambda b, i: (b, i, 0))
    per_b = lambda n: pl.BlockSpec((None, n, KV_W), lambda b, i: (b, 0, 0))
    nkv_col = lambda j: pl.BlockSpec((None, S, KV_W), lambda b, i, j=j: (b, 0, j))
    full = lambda a: pl.BlockSpec(a.shape, lambda b, i: (0,) * a.ndim)
    sub = C // SWA_WINDOW
    swa_k, swa_v = Q_W // KV_W, Q_W // KV_W + 1
    swa_cur = lambda j: pl.BlockSpec((None, C, KV_W), lambda b, i, j=j: (b, i, j))
    swa_prev = lambda j: pl.BlockSpec((None, SWA_WINDOW, KV_W), lambda b, i, j=j: (b, jnp.maximum(i * sub - 1, 0), j))
    smem = pl.BlockSpec(memory_space=pltpu.SMEM)
    return pl.pallas_call(
        body,
        grid=(B, S // C),
        in_specs=[qblk(Q_W), qblk(LANES), per_b(NC), per_b(NC),
                  nkv_col(0), nkv_col(1), nkv_col(2), nkv_col(3),
                  full(kaug), full(caug), full(qfeat), full(ovt),
                  smem, qblk(Q_W), swa_prev(swa_k), swa_cur(swa_k), swa_prev(swa_v), swa_cur(swa_v), full(swa_bias)],
        out_specs=[qblk(Q_W), qblk(Q_W)],
        out_shape=[jax.ShapeDtypeStruct((B, S, Q_W), BF16)] * 2,
        scratch_shapes=[
            pltpu.VMEM((GROUPS, REP * C, 2 * LANES), BF16),
            pltpu.VMEM((GROUPS, REP * C, LANES), F32),
            pltpu.VMEM((2, GROUPS, REP * C, 2 * LANES), F32),
            pltpu.VMEM((GROUPS, REP * C, 2 * LANES), F32),
            pltpu.VMEM((GROUPS, REP * C, LANES), F32),
            pltpu.VMEM((GROUPS, NSB, C), F32),
            pltpu.VMEM((GROUPS, NSB, C), F32),
        ],
        compiler_params=_params(("parallel", "parallel")),
        name="attention",
    )(qn, gates, kcmp, vcmp, nkv, nkv, nkv, nkv, kaug, caug, qfeat, ovt,
      sinks_l2, swa_qkv, swa_qkv, swa_qkv, swa_qkv, swa_qkv, swa_bias)


def _swa_body(sinks_ref, q_ref, kp_ref, kc_ref, vp_ref, vc_ref, bias_ref, out_ref, *, C):
    WB = SWA_WINDOW
    at_start = jnp.where(pl.program_id(1) == 0, 1, 0)
    kall = jnp.concatenate([kp_ref[...], kc_ref[...]], axis=0)
    vall = jnp.concatenate([vp_ref[...], vc_ref[...]], axis=0)
    vall = jnp.concatenate([vall, jnp.ones((WB + C, LANES), BF16)], axis=1)
    lane = lax.broadcasted_iota(jnp.int32, (WB, LANES), 1)
    for sb in range(C // WB):
        q_sb = q_ref[sb * WB:(sb + 1) * WB, :]
        kk = kall[sb * WB:(sb + 2) * WB]
        vv = vall[sb * WB:(sb + 2) * WB]
        table = at_start if sb == 0 else 0
        outs = []
        for g in range(GROUPS):
            s_all = _dot_nt(_group_rows(q_sb, g, WB), kk)
            probs, sink_terms = [], []
            for r in range(REP):
                h = g * REP + r
                s = s_all[r * WB:(r + 1) * WB] + bias_ref[table, h]
                sink = sinks_ref[h]
                m = jnp.maximum(jnp.max(s, axis=-1, keepdims=True), sink)
                probs.append(jnp.exp2(s - m).astype(BF16))
                sink_terms.append(jnp.exp2(sink - m))
            o = _dot(jnp.concatenate(probs, axis=0), vv)
            outs.append([o[r * WB:(r + 1) * WB, :LANES] * (1.0 / (o[r * WB:(r + 1) * WB, LANES:] + sink_terms[r]))
                         for r in range(REP)])
        for r in range(REP):
            chunk = jnp.where(lane < HEAD_DIM, outs[0][r], outs[1][r])
            out_ref[sb * WB:(sb + 1) * WB, r * LANES:(r + 1) * LANES] = chunk.astype(BF16)


def _swa_bias(slopes_l2, C):
    qi = np.arange(C)[:, None]
    kj = np.arange(2 * C)[None, :]
    dist = C + qi - kj
    valid = (dist >= 0) & (dist < SWA_WINDOW)
    tab = -slopes_l2[:, None, None] * jnp.asarray(dist, F32)[None]
    return jnp.stack([jnp.where(valid[None], tab, NEG_INF), jnp.where((valid & (kj >= C))[None], tab, NEG_INF)])


CONV_TAIL = 8
FFN_CHUNK = 256


def _ffn_body(on_ref, os_ref, x_ref, p_ref, wn_ref, ws_ref, g0_ref, g1_ref, wg_ref, wu_ref, cw_ref, cb_ref,
              wd_ref, g2_ref, wple_ref, wpg_ref, out_ref, act_ref, tail_ref, *, tm, tiles_per_seq):
    @pl.when(pl.program_id(0) % tiles_per_seq == 0)
    def _():
        tail_ref[...] = jnp.zeros(tail_ref.shape, F32)

    mix = _dot(on_ref[...], wn_ref[...]) + _dot(os_ref[...], ws_ref[...])
    x1 = x_ref[...] + _rms(mix, g0_ref[...])
    h = _rms(x1, g1_ref[...]).astype(BF16)
    for c in range(D_FF // FFN_CHUNK):
        cols = slice(c * FFN_CHUNK, (c + 1) * FFN_CHUNK)
        a = _dot(h, wg_ref[:, cols])
        u = _dot(h, wu_ref[:, cols])
        ae = jnp.concatenate([tail_ref[:, cols], a], axis=0)
        tail_ref[:, cols] = a[tm - CONV_TAIL:]
        conv = (cw_ref[2:3, cols] * a
                + cw_ref[1:2, cols] * pltpu.roll(ae, 1, 0)[CONV_TAIL:]
                + cw_ref[0:1, cols] * pltpu.roll(ae, 2, 0)[CONV_TAIL:]
                + cb_ref[:, cols])
        act_ref[:, cols] = (_gelu_tanh(conv) * u).astype(BF16)
    y = _dot(act_ref[...], wd_ref[...])
    x2 = x1 + _rms(y, g2_ref[...])
    e = _dot(p_ref[...].astype(BF16), wple_ref[...])
    gate = jax.nn.sigmoid(_dot(x2.astype(BF16), wpg_ref[...]))
    out_ref[...] = x2 + e * gate


def _ffn(o_nsa, o_swa, x2d, p2d, wn, ws, g0, g1, wg, wu, cw, cb, wd, g2, wple, wpg, tm, S):
    T = x2d.shape[0]
    body = functools.partial(_ffn_body, tm=tm, tiles_per_seq=S // tm)
    row = lambda n: pl.BlockSpec((tm, n), lambda i: (i, 0))
    const = lambda a: pl.BlockSpec(a.shape, lambda i: (0,) * a.ndim, pipeline_mode=pl.Buffered(1))
    weights = (wn, ws, g0, g1, wg, wu, cw, cb, wd, g2, wple, wpg)
    return pl.pallas_call(
        body,
        grid=(T // tm,),
        in_specs=[row(Q_W), row(Q_W), row(D_MODEL), row(PLE_DIM)] + [const(a) for a in weights],
        out_specs=row(D_MODEL),
        out_shape=jax.ShapeDtypeStruct((T, D_MODEL), F32),
        scratch_shapes=[pltpu.VMEM((tm, D_FF), BF16), pltpu.VMEM((CONV_TAIL, D_FF), F32)],
        compiler_params=_params(("arbitrary",)),
        name="ffn",
    )(o_nsa, o_swa, x2d, p2d, *weights)


def _alibi_slopes():
    h = jnp.arange(N_HEADS, dtype=F32)
    return jnp.exp2(-8.0 * (h + 1.0) / N_HEADS)


def _key_features(S):
    pos = np.arange(S)
    blk, off = pos // SEL_LEN, pos % SEL_LEN
    feat = np.zeros((S, LANES), np.float32)
    feat[pos, blk] = 1.0
    base = S // SEL_LEN
    assert base <= HEAD_DIM
    for k in range(N_ALIBI_TERMS):
        feat[:, HEAD_DIM + k] = blk * SEL_LEN
        feat[:, HEAD_DIM + N_ALIBI_TERMS + k] = off
    return jnp.asarray(feat, BF16)


def _cmp_features(NC):
    assert NC <= 256
    starts = np.arange(NC) * CMP_STRIDE
    feat = np.zeros((NC, LANES), np.float32)
    for k in range(N_ALIBI_TERMS):
        feat[:, HEAD_DIM + k] = starts
        feat[:, HEAD_DIM + N_ALIBI_TERMS + k] = CMP_LEN - 1
    return jnp.asarray(feat, BF16)


def _query_features(slopes):
    terms, rest = [], slopes
    for _ in range(N_ALIBI_TERMS):
        t = rest.astype(BF16).astype(F32)
        terms.append(t)
        rest = rest - t
    tt = jnp.stack(terms, axis=1)
    feat = jnp.zeros((slopes.shape[0], LANES), F32)
    feat = feat.at[:, HEAD_DIM:HEAD_DIM + N_ALIBI_TERMS].set(tt)
    feat = feat.at[:, HEAD_DIM + N_ALIBI_TERMS:HEAD_DIM + 2 * N_ALIBI_TERMS].set(tt)
    return feat


def _overlap_t(S, NC):
    n = np.arange(NC)[None, :]
    j = np.arange(S // SEL_LEN)[:, None]
    cs, ss = n * CMP_STRIDE, j * SEL_LEN
    ov = (cs < ss + SEL_LEN) & (cs + CMP_LEN - 1 >= ss)
    return jnp.asarray(ov.astype(np.float32), BF16)


def _head_cols(perm):
    return np.concatenate([np.arange(h * HEAD_DIM, (h + 1) * HEAD_DIM) for h in perm])


def kernel(x, p, attn_pre_g, w_in, cmp_pe_k, cmp_w1_k, cmp_w2_k, cmp_pe_v, cmp_w1_v, cmp_w2_v, sinks, w_o,
           attn_post_g, mlp_pre_g, w_gate_up, conv_w, conv_b, w_down, mlp_post_g, w_ple, w_ple_gate):
    B, S, D = x.shape
    T = B * S
    depth = w_in.shape[0]
    slopes = _alibi_slopes()
    swa_slopes, nsa_slopes = slopes[:SWA_HEADS] * LOG2E, slopes[SWA_HEADS:] * LOG2E
    NS = S // CMP_STRIDE
    kaug = _key_features(S)
    caug = _cmp_features(NS)
    qfeat = _query_features(nsa_slopes)
    ovt = _overlap_t(S, NS)
    swa_bias = _swa_bias(swa_slopes, SWA_WINDOW)
    perm = _head_cols(HEAD_PERM)
    scale = HEAD_DIM ** -0.5 * LOG2E
    tm = TOKEN_TILE

    xf = x.reshape(T, D)
    for i in range(depth):
        w = w_in[i]
        g0 = Q_W + 6 * KV_W
        s0 = g0 + GATE_W
        q_n = w[:, :Q_W][:, perm] * scale
        q_s = w[:, s0:s0 + Q_W][:, perm] * scale
        gate_cols = jnp.pad(w[:, g0:s0], ((0, 0), (0, LANES - GATE_W)))
        w_cat = jnp.concatenate([q_n, w[:, Q_W:g0], q_s, w[:, s0 + Q_W:], gate_cols], axis=1).astype(BF16)
        qn, kc, vc, nkv, swa_qkv, gates = _inproj(xf, attn_pre_g[i][None, :], w_cat, tm)

        wk = _compress_weights(cmp_pe_k[i], cmp_w1_k[i], cmp_w2_k[i])
        wv = _compress_weights(cmp_pe_v[i], cmp_w1_v[i], cmp_w2_v[i])
        seg_w = CMP_STRIDE * GROUPS * HEAD_DIM
        kcmp, vcmp = _compress(kc.reshape(B, NS, seg_w), vc.reshape(B, NS, seg_w), wk, wv)

        o_nsa, o_swa = _attention(qn.reshape(B, S, -1), gates.reshape(B, S, -1), kcmp, vcmp,
                                  nkv.reshape(B, S, -1), kaug, caug, qfeat, ovt, sinks[i] * LOG2E,
                                  swa_qkv.reshape(B, S, -1), swa_bias, C=ATTN_TILE, TK=2 * ATTN_TILE)

        wo = w_o[i]
        wgu = w_gate_up[i].astype(BF16)
        xf = _ffn(o_nsa.reshape(T, Q_W), o_swa.reshape(T, Q_W), xf, p[i].reshape(T, PLE_DIM),
                  wo[:Q_W][perm].astype(BF16), wo[Q_W:][perm].astype(BF16), attn_post_g[i][None, :],
                  mlp_pre_g[i][None, :], wgu[:, :D_FF], wgu[:, D_FF:], conv_w[i], conv_b[i][None, :],
                  w_down[i].astype(BF16), mlp_post_g[i][None, :], w_ple[i].astype(BF16),
                  w_ple_gate[i].astype(BF16), tm, S)
    return xf.reshape(B, S, D)
```
